```python
import jax
import jax.numpy as jnp
from jax import lax
import numpy as np

D_MODEL = 1024
BATCH = 16
SEQ = 2048
DEPTH = 2
DEC_BATCH = 128
DEC_SEQ = 8
PAST_LEN = 8192
PAGE_SIZE = 128

N_META = 16
A_HEADS = 8
A_HEAD_DIM = 64
A_ROT = A_HEAD_DIM // 4
A_KV = 2 * A_HEAD_DIM
IDX_HEADS = 8
IDX_DIM = 32
IDX_ROT = IDX_DIM // 4
TOPK_MAX = 256
B_HEADS = 8
Q_LORA = 256
KV_LORA = 128
QK_NOPE = 64
QK_ROPE = 32
V_DIM = 64
MIX_WIDTH = A_HEADS * A_HEAD_DIM + B_HEADS * V_DIM
D_FF = 2816
ROPE_THETA = 500000.0
DEEPNORM_ALPHA = (2 * DEPTH) ** 0.25
DEEPNORM_BETA = (8 * DEPTH) ** -0.25
Q_BLOCK = 128
LN_EPS = 1e-5
RMS_EPS = 1e-6
DSA_ROW = A_KV + IDX_DIM
MLA_ROW = KV_LORA + QK_ROPE
SPLIT_SIZES = (A_HEADS * A_HEAD_DIM, A_KV, IDX_HEADS * IDX_DIM, IDX_DIM, IDX_HEADS, Q_LORA, KV_LORA, QK_ROPE)
SPLIT_POINTS = tuple(int(v) for v in np.cumsum(SPLIT_SIZES)[:-1])
N_IN = int(sum(SPLIT_SIZES))
V_COL_START = A_HEADS * A_HEAD_DIM + A_HEAD_DIM
V_COL_END = A_HEADS * A_HEAD_DIM + A_KV

kernel_name = 'hymba_dsa_mla_macaron_deepnorm_step'


def _layernorm(x, g, b):
    xf = x.astype(jnp.float32)
    mu = jnp.mean(xf, axis=-1, keepdims=True)
    var = jnp.mean(jnp.square(xf - mu), axis=-1, keepdims=True)
    return ((xf - mu) * lax.rsqrt(var + LN_EPS) * g + b).astype(x.dtype)


def _rmsnorm(x, g):
    xf = x.astype(jnp.float32)
    return (xf * lax.rsqrt(jnp.mean(xf * xf, axis=-1, keepdims=True) + RMS_EPS) * g).astype(x.dtype)


def _swiglu(x, w_in, w_out):
    g, u = jnp.split(x @ w_in, 2, axis=-1)
    return (jax.nn.silu(g) * u) @ w_out


def _rope(x, pos, rot):
    half = rot // 2
    inv = ROPE_THETA ** (-jnp.arange(half, dtype=jnp.float32) / half)
    ang = pos.astype(jnp.float32)[:, None] * inv
    ang = ang.reshape((ang.shape[0],) + (1,) * (x.ndim - 3) + (half,))
    cos = jnp.cos(ang).astype(x.dtype)
    sin = jnp.sin(ang).astype(x.dtype)
    x1, x2, rest = x[..., :half], x[..., half:rot], x[..., rot:]
    return jnp.concatenate([x1 * cos - x2 * sin, x2 * cos + x1 * sin, rest], axis=-1)


def _map_query_blocks(fn, xs, n_q):
    qb = min(Q_BLOCK, n_q)
    n_blk = -(-n_q // qb)
    pad = n_blk * qb - n_q

    def split(a):
        a = jnp.pad(a, [(0, 0), (0, pad)] + [(0, 0)] * (a.ndim - 2))
        a = a.reshape((a.shape[0], n_blk, qb) + a.shape[2:])
        return jnp.moveaxis(a, 1, 0)

    out = lax.map(fn, tuple(split(a) for a in xs))
    out = jnp.moveaxis(out, 0, 1)
    out = out.reshape((out.shape[0], n_blk * qb) + out.shape[3:])
    return out[:, :n_q]


def _dsa_attend(q, q_idx, w_idx, q_pos, k_idx, gather_kv, top_k):
    key_pos = jnp.arange(k_idx.shape[1], dtype=jnp.int32)

    def block(args):
        qa, qi, wi, pb = args
        dots = jnp.einsum('bqhd,bkd->bqhk', qi, k_idx, preferred_element_type=jnp.float32)
        score = jnp.einsum('bqh,bqhk->bqk', wi.astype(jnp.float32), jax.nn.relu(dots))
        score = jnp.where(key_pos <= pb[..., None], score, -jnp.inf)
        _, sel = lax.top_k(score, top_k)
        valid = sel <= pb[..., None]
        kv = gather_kv(sel)
        k, v = kv[..., :A_HEAD_DIM], kv[..., A_HEAD_DIM:]
        s = jnp.einsum('bqhd,bqkd->bqhk', qa, k, preferred_element_type=jnp.float32) * (A_HEAD_DIM ** -0.5)
        s = jnp.where(valid[:, :, None, :], s, -jnp.inf)
        p = jax.nn.softmax(s, axis=-1)
        return jnp.einsum('bqhk,bqkd->bqhd', p.astype(v.dtype), v)

    return _map_query_blocks(block, (q, q_idx, w_idx, q_pos), q.shape[1])


def _mla_attend(q_lat, q_pe, q_pos, c_kv, k_pe):
    key_pos = jnp.arange(c_kv.shape[1], dtype=jnp.int32)
    scale = (QK_NOPE + QK_ROPE) ** -0.5

    def block(args):
        ql, qp, pb = args
        s = (jnp.einsum('bqhr,bkr->bhqk', ql, c_kv, preferred_element_type=jnp.float32)
             + jnp.einsum('bqhd,bkd->bhqk', qp, k_pe, preferred_element_type=jnp.float32)) * scale
        s = jnp.where(key_pos <= pb[:, None, :, None], s, -jnp.inf)
        p = jax.nn.softmax(s, axis=-1)
        return jnp.einsum('bhqk,bkr->bqhr', p.astype(c_kv.dtype), c_kv)

    return _map_query_blocks(block, (q_lat, q_pe, q_pos), q_lat.shape[1])


def _project(h, pos, w_in_l, q_norm_l, w_uq_l, kv_norm_l, w_uk_l):
    b, t, _ = h.shape
    z = h @ w_in_l
    a_q, a_kv, i_q, i_k, i_w, b_cq, b_ckv, b_kpe = jnp.split(z, SPLIT_POINTS, axis=-1)
    a_q = _rope(a_q.reshape(b, t, A_HEADS, A_HEAD_DIM), pos, A_ROT)
    a_k = _rope(a_kv[..., :A_HEAD_DIM], pos, A_ROT)
    a_v = a_kv[..., A_HEAD_DIM:]
    i_q = _rope(i_q.reshape(b, t, IDX_HEADS, IDX_DIM), pos, IDX_ROT)
    i_k = _rope(i_k, pos, IDX_ROT)
    i_w = i_w * ((IDX_HEADS * IDX_DIM) ** -0.5)
    qb = (_rmsnorm(b_cq, q_norm_l) @ w_uq_l).reshape(b, t, B_HEADS, QK_NOPE + QK_ROPE)
    q_nope = qb[..., :QK_NOPE]
    q_pe = _rope(qb[..., QK_NOPE:], pos, QK_ROPE)
    q_lat = jnp.einsum('bthn,rhn->bthr', q_nope, w_uk_l)
    c_kv = _rmsnorm(b_ckv, kv_norm_l)
    k_pe = _rope(b_kpe, pos, QK_ROPE)
    dsa_row = jnp.concatenate([a_k, a_v, i_k], axis=-1)
    mla_row = jnp.concatenate([c_kv, k_pe], axis=-1)
    return a_q, i_q, i_w, q_lat, q_pe, dsa_row, mla_row


def _merge(o_a, o_lat, w_uv_l, w_o_l):
    b, t = o_a.shape[:2]
    o_b = jnp.einsum('bthr,rhd->bthd', o_lat, w_uv_l)
    o = jnp.concatenate([o_a.reshape(b, t, -1), o_b.reshape(b, t, -1)], axis=-1)
    return o @ w_o_l


def _mix_prompt(h, mw):
    w_in_l, q_norm_l, w_uq_l, kv_norm_l, w_uk_l, w_uv_l, w_o_l = mw
    b, t, _ = h.shape
    pos = jnp.arange(t, dtype=jnp.int32)
    a_q, i_q, i_w, q_lat, q_pe, dsa_row, mla_row = _project(h, pos, w_in_l, q_norm_l, w_uq_l, kv_norm_l, w_uk_l)
    q_pos = jnp.broadcast_to(pos, (b, t))
    kv_rows = dsa_row[..., :A_KV]

    def gather_kv(sel):
        return jax.vmap(lambda kv, s: kv[s])(kv_rows, sel)

    top_k = min(TOPK_MAX, t // 4)
    o_a = _dsa_attend(a_q, i_q, i_w, q_pos, dsa_row[..., A_KV:], gather_kv, top_k)
    o_lat = _mla_attend(q_lat, q_pe, q_pos, mla_row[..., :KV_LORA], mla_row[..., KV_LORA:])
    return _merge(o_a, o_lat, w_uv_l, w_o_l), dsa_row, mla_row


def _mix_sample(h, cache_dsa_l, cache_mla_l, page_table, mw):
    w_in_l, q_norm_l, w_uq_l, kv_norm_l, w_uk_l, w_uv_l, w_o_l = mw
    b, t, _ = h.shape
    past_len = page_table.shape[1] * PAGE_SIZE
    pos = past_len + jnp.arange(t, dtype=jnp.int32)
    a_q, i_q, i_w, q_lat, q_pe, dsa_row, mla_row = _project(h, pos, w_in_l, q_norm_l, w_uq_l, kv_norm_l, w_uk_l)
    q_pos = jnp.broadcast_to(pos, (b, t))
    idx_past = cache_dsa_l[:, :, A_KV:][page_table].reshape(b, past_len, IDX_DIM)
    k_idx = jnp.concatenate([idx_past, dsa_row[..., A_KV:]], axis=1)
    new_kv = dsa_row[..., :A_KV]

    def gather_kv(sel):
        def one(pt, nkv, s):
            sp = jnp.minimum(s, past_len - 1)
            past = cache_dsa_l[pt[sp // PAGE_SIZE], sp % PAGE_SIZE, :A_KV]
            new = nkv[jnp.clip(s - past_len, 0, t - 1)]
            return jnp.where((s < past_len)[..., None], past, new)
        return jax.vmap(one)(page_table, new_kv, sel)

    top_k = min(TOPK_MAX, (past_len + t) // 4)
    o_a = _dsa_attend(a_q, i_q, i_w, q_pos, k_idx, gather_kv, top_k)
    mla_past = cache_mla_l[page_table].reshape(b, past_len, MLA_ROW)
    mla_all = jnp.concatenate([mla_past, mla_row], axis=1)
    o_lat = _mla_attend(q_lat, q_pe, q_pos, mla_all[..., :KV_LORA], mla_all[..., KV_LORA:])
    return _merge(o_a, o_lat, w_uv_l, w_o_l), dsa_row, mla_row


def setup_inputs(seed: int = 0) -> dict:
    key = jax.random.key(seed)
    ks = jax.random.split(key, 24)
    n_pages = PAST_LEN // PAGE_SIZE
    n_pool = (DEC_BATCH * n_pages * 5) // 4

    def nrm(k, shape, scale):
        return jax.random.normal(k, shape, jnp.float32) * scale

    def gain(k, shape):
        return 1.0 + 0.02 * jax.random.normal(k, shape, jnp.float32)

    page_table = jax.random.permutation(ks[4], n_pool)[: DEC_BATCH * n_pages].reshape(DEC_BATCH, n_pages).astype(jnp.int32)
    w_in = nrm(ks[10], (DEPTH, D_MODEL, N_IN), D_MODEL ** -0.5)
    w_in = w_in.at[:, :, V_COL_START:V_COL_END].multiply(DEEPNORM_BETA)
    return {
        'x_prompt': nrm(ks[0], (BATCH, SEQ, D_MODEL), 1.0),
        'x_sample': nrm(ks[1], (DEC_BATCH, DEC_SEQ, D_MODEL), 1.0),
        'cache_dsa': nrm(ks[2], (DEPTH, n_pool, PAGE_SIZE, DSA_ROW), 1.0),
        'cache_mla': nrm(ks[3], (DEPTH, n_pool, PAGE_SIZE, MLA_ROW), 1.0),
        'page_table': page_table,
        'meta_tokens': nrm(ks[5], (N_META, D_MODEL), 1.0),
        'ln1_g': gain(ks[6], (DEPTH, D_MODEL)),
        'ln1_b': nrm(ks[7], (DEPTH, D_MODEL), 0.02),
        'ffn1_w_in': nrm(ks[8], (DEPTH, D_MODEL, 2 * D_FF), D_MODEL ** -0.5),
        'ffn1_w_out': nrm(ks[9], (DEPTH, D_FF, D_MODEL), D_FF ** -0.5 * DEEPNORM_BETA),
        'w_in': w_in,
        'mla_q_norm': gain(ks[11], (DEPTH, Q_LORA)),
        'mla_w_uq': nrm(ks[12], (DEPTH, Q_LORA, B_HEADS * (QK_NOPE + QK_ROPE)), Q_LORA ** -0.5),
        'mla_kv_norm': gain(ks[13], (DEPTH, KV_LORA)),
        'mla_w_uk': nrm(ks[14], (DEPTH, KV_LORA, B_HEADS, QK_NOPE), KV_LORA ** -0.5),
        'mla_w_uv': nrm(ks[15], (DEPTH, KV_LORA, B_HEADS, V_DIM), KV_LORA ** -0.5 * DEEPNORM_BETA),
        'w_o': nrm(ks[16], (DEPTH, MIX_WIDTH, D_MODEL), MIX_WIDTH ** -0.5 * DEEPNORM_BETA),
        'ln2_g': gain(ks[17], (DEPTH, D_MODEL)),
        'ln2_b': nrm(ks[18], (DEPTH, D_MODEL), 0.02),
        'ffn2_w_in': nrm(ks[19], (DEPTH, D_MODEL, 2 * D_FF), D_MODEL ** -0.5),
        'ffn2_w_out': nrm(ks[20], (DEPTH, D_FF, D_MODEL), D_FF ** -0.5 * DEEPNORM_BETA),
        'ln3_g': gain(ks[21], (DEPTH, D_MODEL)),
        'ln3_b': nrm(ks[22], (DEPTH, D_MODEL), 0.02),
    }


def reference(x_prompt, x_sample, cache_dsa, cache_mla, page_table, meta_tokens,
              ln1_g, ln1_b, ffn1_w_in, ffn1_w_out, w_in, mla_q_norm, mla_w_uq, mla_kv_norm,
              mla_w_uk, mla_w_uv, w_o, ln2_g, ln2_b, ffn2_w_in, ffn2_w_out, ln3_g, ln3_b):
    b = x_prompt.shape[0]
    meta = jnp.broadcast_to(meta_tokens[None].astype(x_prompt.dtype), (b, N_META, D_MODEL))
    xp = jnp.concatenate([meta, x_prompt], axis=1)
    xs = x_sample
    dsa_p, mla_p, dsa_s, mla_s = [], [], [], []
    for l in range(DEPTH):
        mw = (w_in[l], mla_q_norm[l], mla_w_uq[l], mla_kv_norm[l], mla_w_uk[l], mla_w_uv[l], w_o[l])
        xp = _layernorm(DEEPNORM_ALPHA * xp + 0.5 * _swiglu(xp, ffn1_w_in[l], ffn1_w_out[l]), ln1_g[l], ln1_b[l])
        xs = _layernorm(DEEPNORM_ALPHA * xs + 0.5 * _swiglu(xs, ffn1_w_in[l], ffn1_w_out[l]), ln1_g[l], ln1_b[l])
        mix_p, rows_dp, rows_mp = _mix_prompt(xp, mw)
        mix_s, rows_ds, rows_ms = _mix_sample(xs, cache_dsa[l], cache_mla[l], page_table, mw)
        xp = _layernorm(DEEPNORM_ALPHA * xp + mix_p, ln2_g[l], ln2_b[l])
        xs = _layernorm(DEEPNORM_ALPHA * xs + mix_s, ln2_g[l], ln2_b[l])
        xp = _layernorm(DEEPNORM_ALPHA * xp + 0.5 * _swiglu(xp, ffn2_w_in[l], ffn2_w_out[l]), ln3_g[l], ln3_b[l])
        xs = _layernorm(DEEPNORM_ALPHA * xs + 0.5 * _swiglu(xs, ffn2_w_in[l], ffn2_w_out[l]), ln3_g[l], ln3_b[l])
        dsa_p.append(rows_dp)
        mla_p.append(rows_mp)
        dsa_s.append(rows_ds)
        mla_s.append(rows_ms)
    y_prompt = xp[:, N_META:]
    return (y_prompt, xs, jnp.stack(dsa_p), jnp.stack(mla_p), jnp.stack(dsa_s), jnp.stack(mla_s))
```

```python
import functools

import numpy as np
import jax
import jax.numpy as jnp
from jax import lax
from jax.experimental import pallas as pl
from jax.experimental.pallas import tpu as pltpu

D_MODEL = 1024
DEPTH = 2
PAGE_SIZE = 128
N_META = 16
A_HEADS = 8
A_HEAD_DIM = 64
A_ROT = 16
A_KV = 128
IDX_HEADS = 8
IDX_DIM = 32
IDX_ROT = 8
TOPK_MAX = 256
B_HEADS = 8
Q_LORA = 256
KV_LORA = 128
QK_NOPE = 64
QK_ROPE = 32
V_DIM = 64
D_FF = 2816
ROPE_THETA = 500000.0
DEEPNORM_ALPHA = (2 * DEPTH) ** 0.25
LN_EPS = 1e-5
RMS_EPS = 1e-6
ROW = 160

BF16 = jnp.bfloat16
F32 = jnp.float32
INT_MIN = -(2 ** 31)
NEG_INF = float("-inf")

VMEM_LIMIT = 56 * 1024 * 1024

C_AQ = 0
C_IQ = 512
C_CQ = 768
C_AKV = 1024
C_IK = 1152
C_CKV = 1280
C_KPE = 1408
C_IW = 1440
N1P = 1536
C_PE = N1P
N_TAB = N1P + B_HEADS * QK_ROPE

O_AQ, O_AKV, O_IQ, O_IK, O_IW, O_CQ, O_CKV, O_KPE = 0, 512, 640, 896, 928, 936, 1192, 1320


def _layout_tables():
    src = np.full((N1P,), -1, np.int64)
    for new, old, n in ((C_AQ, O_AQ, 512), (C_IQ, O_IQ, 256), (C_CQ, O_CQ, 256), (C_AKV, O_AKV, 128),
                        (C_IK, O_IK, 32), (C_CKV, O_CKV, 128), (C_KPE, O_KPE, 32), (C_IW, O_IW, 8)):
        src[new:new + n] = np.arange(old, old + n)
    partner = np.arange(N_TAB)
    sign = np.zeros((N_TAB,), np.float32)
    fidx = np.full((N_TAB,), -1, np.int64)
    half_of = np.ones((N_TAB,), np.int64)
    for start, heads, hd, rot in ((C_AQ, A_HEADS, A_HEAD_DIM, A_ROT), (C_AKV, 1, A_HEAD_DIM, A_ROT),
                                  (C_IQ, IDX_HEADS, IDX_DIM, IDX_ROT), (C_IK, 1, IDX_DIM, IDX_ROT),
                                  (C_KPE, 1, QK_ROPE, QK_ROPE), (C_PE, B_HEADS, QK_ROPE, QK_ROPE)):
        half = rot // 2
        for h in range(heads):
            for d in range(rot):
                c = start + h * hd + d
                if d < half:
                    partner[c], sign[c] = c + half, -1.0
                else:
                    partner[c], sign[c] = c - half, 1.0
                fidx[c] = d % half
                half_of[c] = half
    return src, partner, sign, fidx, half_of


_SRC, _PARTNER, _SIGN, _FIDX, _HALF = _layout_tables()


def _rope_freqs():
    f = jnp.zeros((N_TAB,), F32)
    for half in (A_ROT // 2, IDX_ROT // 2, QK_ROPE // 2):
        inv = ROPE_THETA ** (-jnp.arange(half, dtype=F32) / half)
        sel = (_HALF == half) & (_FIDX >= 0)
        cols = np.nonzero(sel)[0]
        f = f.at[cols].set(inv[_FIDX[cols]])
    return f.reshape(1, N_TAB)


def _cparams(*sem):
    return pltpu.CompilerParams(dimension_semantics=sem, vmem_limit_bytes=VMEM_LIMIT)


def _resident(shape):
    nd = len(shape)
    return pl.BlockSpec(shape, lambda *_: (0,) * nd, pipeline_mode=pl.Buffered(1))


def _layernorm(y, g, b):
    mu = jnp.mean(y, axis=-1, keepdims=True)
    yc = y - mu
    var = jnp.mean(yc * yc, axis=-1, keepdims=True)
    return yc * lax.rsqrt(var + LN_EPS) * g + b


def _rmsnorm(y, g):
    return y * lax.rsqrt(jnp.mean(y * y, axis=-1, keepdims=True) + RMS_EPS) * g


def _dot(a, b):
    return jnp.dot(a, b, preferred_element_type=F32)


def _dot_nt(a, b):
    return lax.dot_general(a, b, (((1,), (1,)), ((), ())), preferred_element_type=F32)


def _rope_table_kernel(pos_ref, freq_ref, cos_ref, sin_ref):
    ang = pos_ref[...] * freq_ref[...]
    cos_ref[...] = jnp.cos(ang)
    sin_ref[...] = jnp.sin(ang)


def _rope_table(pos, freq, tr):
    r, n = pos.shape[0], freq.shape[1]
    return pl.pallas_call(
        _rope_table_kernel,
        grid=(r // tr,),
        in_specs=[pl.BlockSpec((tr, 1), lambda i: (i, 0)), pl.BlockSpec((1, n), lambda i: (0, 0))],
        out_specs=[pl.BlockSpec((tr, n), lambda i: (i, 0))] * 2,
        out_shape=[jax.ShapeDtypeStruct((r, n), F32)] * 2,
        compiler_params=_cparams("parallel"),
        name="rope_table",
    )(pos, freq)


FF_CHUNK = 256


def _ffn_kernel(x_ref, wg_ref, wu_ref, wo_ref, g_ref, b_ref, o_ref, acc_ref):
    x = x_ref[...]
    xb = x.astype(BF16)
    for c in range(D_FF // FF_CHUNK):
        sl = slice(c * FF_CHUNK, (c + 1) * FF_CHUNK)
        gate = _dot(xb, wg_ref[:, sl])
        up = _dot(xb, wu_ref[:, sl])
        hid = (gate * jax.nn.sigmoid(gate) * up).astype(BF16)
        part = _dot(hid, wo_ref[sl, :])
        if c == 0:
            acc_ref[...] = part
        else:
            acc_ref[...] += part
    y = DEEPNORM_ALPHA * x + 0.5 * acc_ref[...]
    o_ref[...] = _layernorm(y, g_ref[...], b_ref[...])


def _ffn(x, wg, wu, wo, g, b, tm):
    n = x.shape[0]
    row = pl.BlockSpec((tm, D_MODEL), lambda i: (i, 0))
    return pl.pallas_call(
        _ffn_kernel,
        grid=(n // tm,),
        in_specs=[row, _resident(wg.shape), _resident(wu.shape), _resident(wo.shape),
                  _resident(g.shape), _resident(b.shape)],
        out_specs=row,
        out_shape=jax.ShapeDtypeStruct((n, D_MODEL), F32),
        scratch_shapes=[pltpu.VMEM((tm, D_MODEL), F32)],
        compiler_params=_cparams("parallel"),
        name="ffn",
    )(x, wg, wu, wo, g, b)


def _proj_kernel(x_ref, cos_ref, sin_ref, w1_ref, w1r_ref, qn_ref, wuq_ref, wuqr_ref, kvn_ref, wuk_ref,
                 qa_ref, qi_ref, wi_ref, qcat_ref, dsa_ref, mla_ref):
    xb = x_ref[...].astype(BF16)
    cos = cos_ref[...]
    sin = sin_ref[...]
    z = _dot(xb, w1_ref[...]) * cos[:, :N1P] + _dot(xb, w1r_ref[...]) * sin[:, :N1P]
    for h in range(A_HEADS):
        qa_ref[h] = z[:, C_AQ + h * A_HEAD_DIM:C_AQ + (h + 1) * A_HEAD_DIM]
    for h in range(IDX_HEADS):
        qi_ref[h] = z[:, C_IQ + h * IDX_DIM:C_IQ + (h + 1) * IDX_DIM]
    wi_ref[...] = z[:, C_IW:C_IW + IDX_HEADS] * ((IDX_HEADS * IDX_DIM) ** -0.5)
    dsa_ref[...] = z[:, C_AKV:C_AKV + ROW]
    mla_ref[:, :KV_LORA] = _rmsnorm(z[:, C_CKV:C_CKV + KV_LORA], kvn_ref[...])
    mla_ref[:, KV_LORA:] = z[:, C_KPE:C_KPE + QK_ROPE]
    cq = _rmsnorm(z[:, C_CQ:C_CQ + Q_LORA], qn_ref[...]).astype(BF16)
    qb = _dot(cq, wuq_ref[...])
    n_nope = B_HEADS * QK_NOPE
    q_pe = qb[:, n_nope:] * cos[:, C_PE:] + _dot(cq, wuqr_ref[...]) * sin[:, C_PE:]
    for h in range(B_HEADS):
        q_nope = qb[:, h * QK_NOPE:(h + 1) * QK_NOPE].astype(BF16)
        qcat_ref[h, :, :KV_LORA] = _dot(q_nope, wuk_ref[h])
        qcat_ref[h, :, KV_LORA:] = q_pe[:, h * QK_ROPE:(h + 1) * QK_ROPE]


def _project(x, cos, sin, tab_blocks, w, tm):
    n = x.shape[0]
    w1, w1r, qn, wuq, wuqr, kvn, wukt = w
    row = lambda width: pl.BlockSpec((tm, width), lambda i: (i, 0))
    heads = lambda width: pl.BlockSpec((8, tm, width), lambda i: (0, i, 0))
    tab = pl.BlockSpec((tm, N_TAB), lambda i: (i % tab_blocks, 0))
    return pl.pallas_call(
        _proj_kernel,
        grid=(n // tm,),
        in_specs=[row(D_MODEL), tab, tab, _resident(w1.shape), _resident(w1r.shape), _resident(qn.shape),
                  _resident(wuq.shape), _resident(wuqr.shape), _resident(kvn.shape), _resident(wukt.shape)],
        out_specs=[heads(A_HEAD_DIM), heads(IDX_DIM), row(IDX_HEADS), heads(ROW), row(ROW), row(ROW)],
        out_shape=[jax.ShapeDtypeStruct((A_HEADS, n, A_HEAD_DIM), F32),
                   jax.ShapeDtypeStruct((IDX_HEADS, n, IDX_DIM), F32),
                   jax.ShapeDtypeStruct((n, IDX_HEADS), F32),
                   jax.ShapeDtypeStruct((B_HEADS, n, ROW), F32),
                   jax.ShapeDtypeStruct((n, ROW), F32),
                   jax.ShapeDtypeStruct((n, ROW), F32)],
        compiler_params=_cparams("parallel"),
        name="project",
    )(x, cos, sin, w1, w1r, qn, wuq, wuqr, kvn, wukt)


def _merge_kernel(x_ref, oa_ref, olat_ref, wuv_ref, woa_ref, wob_ref, g_ref, b_ref, o_ref):
    ob = [_dot(olat_ref[:, h * KV_LORA:(h + 1) * KV_LORA].astype(BF16), wuv_ref[h]) for h in range(B_HEADS)]
    ob = jnp.concatenate(ob, axis=-1).astype(BF16)
    mix = _dot(oa_ref[...].astype(BF16), woa_ref[...]) + _dot(ob, wob_ref[...])
    o_ref[...] = _layernorm(DEEPNORM_ALPHA * x_ref[...] + mix, g_ref[...], b_ref[...])


def _merge(x, oa, olat, wuv, woa, wob, g, b, tm):
    n = x.shape[0]
    row = lambda width: pl.BlockSpec((tm, width), lambda i: (i, 0))
    return pl.pallas_call(
        _merge_kernel,
        grid=(n // tm,),
        in_specs=[row(D_MODEL), row(A_HEADS * A_HEAD_DIM), row(B_HEADS * KV_LORA), _resident(wuv.shape),
                  _resident(woa.shape), _resident(wob.shape), _resident(g.shape), _resident(b.shape)],
        out_specs=row(D_MODEL),
        out_shape=jax.ShapeDtypeStruct((n, D_MODEL), F32),
        compiler_params=_cparams("parallel"),
        name="merge",
    )(x, oa, olat, wuv, woa, wob, g, b)


LANE = 128
SELECT_BITS = 32


def _order_key(score):
    bits = lax.bitcast_convert_type(score + 0.0, jnp.int32)
    return bits ^ ((bits >> 31) & jnp.int32(0x7FFFFFFF))


def _count_ge(keys, cand):
    return jnp.sum(jnp.where(keys >= cand, 1.0, 0.0), axis=1, keepdims=True)


def _kth_largest_key(key_ref, top_k):
    rows = key_ref.shape[0]

    def body(it, t):
        bit = lax.shift_left(jnp.int32(1), jnp.int32(SELECT_BITS - 1) - it)
        cand = t ^ bit
        return jnp.where(_count_ge(key_ref[...], cand) >= top_k, cand, t)

    return lax.fori_loop(0, SELECT_BITS, body, jnp.full((rows, 1), INT_MIN, jnp.int32))


def _select_bias(key_ref, bias_ref, top_k, lane_order):
    kth = jnp.maximum(_kth_largest_key(key_ref, top_k), jnp.int32(INT_MIN + 1))
    keys = key_ref[...]
    bias_ref[...] = jnp.where(keys >= kth, 0.0, NEG_INF)
    n_ge = _count_ge(keys, kth)

    @pl.when(jnp.max(n_ge) > top_k)
    def _():
        ks = key_ref[...]
        need = top_k - jnp.sum(jnp.where(ks > kth, 1.0, 0.0), axis=1, keepdims=True)
        r_i = lax.broadcasted_iota(jnp.int32, (LANE, LANE), 0)
        c_i = lax.broadcasted_iota(jnp.int32, (LANE, LANE), 1)
        tri = jnp.where(r_i <= c_i, 1.0, 0.0).astype(BF16)
        seen = jnp.zeros_like(need)
        for blk in lane_order:
            sl = slice(blk * LANE, (blk + 1) * LANE)
            kb = key_ref[:, sl]
            tie = kb == kth
            tie_f = jnp.where(tie, 1.0, 0.0)
            rank = _dot(tie_f.astype(BF16), tri) + seen
            keep = (kb > kth) | (tie & (rank <= need))
            bias_ref[:, sl] = jnp.where(keep, 0.0, NEG_INF)
            seen = seen + jnp.sum(tie_f, axis=1, keepdims=True)


def _softmax_rows(s):
    m = jnp.max(s, axis=-1, keepdims=True)
    p = jnp.exp(s - m)
    return p, 1.0 / jnp.sum(p, axis=-1, keepdims=True)


def _prompt_positions(tq, n_real, q_pos0):
    lk = n_real + LANE
    lane = lax.broadcasted_iota(jnp.int32, (1, lk), 1)
    big = jnp.int32(2 ** 30)
    key_pos = jnp.where(lane < n_real, lane + N_META, jnp.where(lane < n_real + N_META, lane - n_real, big))
    q_pos = lax.broadcasted_iota(jnp.int32, (tq, 1), 0) + q_pos0
    return key_pos <= q_pos


def _gather_keys(real_ref, meta_ref, n_real):
    if n_real == 0:
        return meta_ref[...].astype(BF16)
    return jnp.concatenate([real_ref[...], meta_ref[...]], axis=0).astype(BF16)


def _dsa_prompt_kernel(qa_ref, qi_ref, wi_ref, real_ref, meta_ref, o_ref, key_ref, bias_ref, *,
                       tq, n_real, q_pos0, top_k):
    kb = _gather_keys(real_ref, meta_ref, n_real)
    k_idx = kb[:, A_KV:]
    wi = wi_ref[...]
    score = None
    for h in range(IDX_HEADS):
        d = jnp.maximum(_dot_nt(qi_ref[h].astype(BF16), k_idx), 0.0) * wi[:, h:h + 1]
        score = d if score is None else score + d
    adm = _prompt_positions(tq, n_real, q_pos0)
    key_ref[...] = jnp.where(adm, _order_key(score), jnp.int32(INT_MIN))
    n_blk = n_real // LANE
    _select_bias(key_ref, bias_ref, top_k, [n_blk] + list(range(n_blk)))
    k_att = kb[:, :A_HEAD_DIM]
    v_att = kb[:, A_HEAD_DIM:A_KV]
    for h in range(A_HEADS):
        s = _dot_nt(qa_ref[h].astype(BF16), k_att) * (A_HEAD_DIM ** -0.5) + bias_ref[...]
        p, inv = _softmax_rows(s)
        o_ref[:, h * A_HEAD_DIM:(h + 1) * A_HEAD_DIM] = _dot(p.astype(BF16), v_att) * inv


def _mla_prompt_kernel(q_ref, real_ref, meta_ref, o_ref, *, tq, n_real, q_pos0):
    kb = _gather_keys(real_ref, meta_ref, n_real)
    c_kv = kb[:, :KV_LORA]
    bias = jnp.where(_prompt_positions(tq, n_real, q_pos0), 0.0, NEG_INF)
    scale = (QK_NOPE + QK_ROPE) ** -0.5
    for h in range(B_HEADS):
        s = _dot_nt(q_ref[h].astype(BF16), kb) * scale + bias
        p, inv = _softmax_rows(s)
        o_ref[:, h * KV_LORA:(h + 1) * KV_LORA] = _dot(p.astype(BF16), c_kv) * inv


def _prompt_attention(qa, qi, wi, qcat, dsa_real, mla_real, dsa_meta, mla_meta, *, blk, tq, q_pos0, top_k):
    nb = qa.shape[1]
    has_real = dsa_real is not None
    n_real = (blk + 1) * tq if has_real else 0
    lk = n_real + LANE
    if not has_real:
        dsa_real = jnp.zeros((nb, 8, ROW), F32)
        mla_real = dsa_real
    nr_blk = max(n_real, 8)
    heads = lambda width: pl.BlockSpec((8, None, tq, width), lambda b: (0, b, blk, 0))
    rows = lambda width: pl.BlockSpec((None, tq, width), lambda b: (b, blk, 0))
    real = pl.BlockSpec((None, nr_blk, ROW), lambda b: (b, 0, 0))
    meta = _resident((LANE, ROW))
    pos0 = q_pos0 + blk * tq
    o_a = pl.pallas_call(
        functools.partial(_dsa_prompt_kernel, tq=tq, n_real=n_real, q_pos0=pos0, top_k=top_k),
        grid=(nb,),
        in_specs=[heads(A_HEAD_DIM), heads(IDX_DIM), rows(IDX_HEADS), real, meta],
        out_specs=pl.BlockSpec((None, tq, A_HEADS * A_HEAD_DIM), lambda b: (b, 0, 0)),
        out_shape=jax.ShapeDtypeStruct((nb, tq, A_HEADS * A_HEAD_DIM), F32),
        scratch_shapes=[pltpu.VMEM((tq, lk), jnp.int32), pltpu.VMEM((tq, lk), F32)],
        compiler_params=_cparams("parallel"),
        name=f"dsa_prompt_{n_real}",
    )(qa, qi, wi, dsa_real, dsa_meta)
    o_lat = pl.pallas_call(
        functools.partial(_mla_prompt_kernel, tq=tq, n_real=n_real, q_pos0=pos0),
        grid=(nb,),
        in_specs=[heads(ROW), real, meta],
        out_specs=pl.BlockSpec((None, tq, B_HEADS * KV_LORA), lambda b: (b, 0, 0)),
        out_shape=jax.ShapeDtypeStruct((nb, tq, B_HEADS * KV_LORA), F32),
        compiler_params=_cparams("parallel"),
        name=f"mla_prompt_{n_real}",
    )(qcat, mla_real, mla_meta)
    return o_a, o_lat


def _sample_kernel(pt_ref, qa_ref, qi_ref, wi_ref, qcat_ref, dsa_new_ref, mla_new_ref, dsa_hbm, mla_hbm,
                   oa_ref, olat_ref, dsa_buf, mla_buf, sems, key_ref, bias_ref, *, layer, n_pages, t_new, top_k):
    b = pl.program_id(0)
    nb = pl.num_programs(0)
    slot = b % 2
    past = n_pages * PAGE_SIZE
    lk = past + PAGE_SIZE

    def page_copies(seq, slt):
        out = []
        for p in range(n_pages):
            page = pt_ref[seq * n_pages + p]
            out.append(pltpu.make_async_copy(dsa_hbm.at[layer, page], dsa_buf.at[slt, p], sems.at[0, slt]))
            out.append(pltpu.make_async_copy(mla_hbm.at[layer, page], mla_buf.at[slt, p], sems.at[1, slt]))
        return out

    @pl.when(b == 0)
    def _():
        for c in page_copies(b, slot):
            c.start()

    @pl.when(b + 1 < nb)
    def _():
        for c in page_copies(b + 1, 1 - slot):
            c.start()

    pad = jnp.zeros((PAGE_SIZE - t_new, ROW), F32)
    dsa_buf[slot, n_pages] = jnp.concatenate([dsa_new_ref[...], pad], axis=0)
    mla_buf[slot, n_pages] = jnp.concatenate([mla_new_ref[...], pad], axis=0)

    for c in page_copies(b, slot):
        c.wait()

    n_row = A_HEADS * t_new
    kd = dsa_buf[slot].reshape(lk, ROW).astype(BF16)
    lane = lax.broadcasted_iota(jnp.int32, (1, lk), 1)
    q_id = lax.broadcasted_iota(jnp.int32, (t_new, 1), 0)
    adm = lane <= q_id + past
    causal_bias = jnp.where(adm, 0.0, NEG_INF)

    d = jnp.maximum(_dot_nt(qi_ref[...].reshape(n_row, IDX_DIM).astype(BF16), kd[:, A_KV:]), 0.0)
    d = d * wi_ref[...]
    score = d[0:t_new]
    for h in range(1, IDX_HEADS):
        score = score + d[h * t_new:(h + 1) * t_new]
    key_ref[...] = jnp.where(adm, _order_key(score), jnp.int32(INT_MIN))
    _select_bias(key_ref, bias_ref, top_k, list(range(lk // LANE)))

    def per_head(bias):
        return jnp.concatenate([bias] * A_HEADS, axis=0)

    s = _dot_nt(qa_ref[...].reshape(n_row, A_HEAD_DIM).astype(BF16), kd[:, :A_HEAD_DIM]) * (A_HEAD_DIM ** -0.5)
    p, inv = _softmax_rows(s + per_head(bias_ref[...]))
    o = _dot(p.astype(BF16), kd[:, A_HEAD_DIM:A_KV]) * inv
    for h in range(A_HEADS):
        oa_ref[:, h * A_HEAD_DIM:(h + 1) * A_HEAD_DIM] = o[h * t_new:(h + 1) * t_new]

    km = mla_buf[slot].reshape(lk, ROW).astype(BF16)
    s = _dot_nt(qcat_ref[...].reshape(n_row, ROW).astype(BF16), km) * ((QK_NOPE + QK_ROPE) ** -0.5)
    p, inv = _softmax_rows(s + per_head(causal_bias))
    o = _dot(p.astype(BF16), km[:, :KV_LORA]) * inv
    for h in range(B_HEADS):
        olat_ref[:, h * KV_LORA:(h + 1) * KV_LORA] = o[h * t_new:(h + 1) * t_new]


def _sample_attention(page_table, qa, qi, wi_col, qcat, dsa_new, mla_new, cache_dsa, cache_mla, *, layer, top_k):
    nb, n_pages = page_table.shape
    t_new = dsa_new.shape[1]
    lk = (n_pages + 1) * PAGE_SIZE
    heads = lambda width: pl.BlockSpec((8, None, t_new, width), lambda b, pt: (0, b, 0, 0))
    rows = lambda n, width: pl.BlockSpec((None, n, width), lambda b, pt: (b, 0, 0))
    grid_spec = pltpu.PrefetchScalarGridSpec(
        num_scalar_prefetch=1,
        grid=(nb,),
        in_specs=[heads(A_HEAD_DIM), heads(IDX_DIM), rows(8 * t_new, 1), heads(ROW), rows(t_new, ROW),
                  rows(t_new, ROW), pl.BlockSpec(memory_space=pl.ANY), pl.BlockSpec(memory_space=pl.ANY)],
        out_specs=[rows(t_new, A_HEADS * A_HEAD_DIM), rows(t_new, B_HEADS * KV_LORA)],
        scratch_shapes=[pltpu.VMEM((2, n_pages + 1, PAGE_SIZE, ROW), F32),
                        pltpu.VMEM((2, n_pages + 1, PAGE_SIZE, ROW), F32),
                        pltpu.SemaphoreType.DMA((2, 2)),
                        pltpu.VMEM((t_new, lk), jnp.int32),
                        pltpu.VMEM((t_new, lk), F32)],
    )
    return pl.pallas_call(
        functools.partial(_sample_kernel, layer=layer, n_pages=n_pages, t_new=t_new, top_k=top_k),
        grid_spec=grid_spec,
        out_shape=[jax.ShapeDtypeStruct((nb, t_new, A_HEADS * A_HEAD_DIM), F32),
                   jax.ShapeDtypeStruct((nb, t_new, B_HEADS * KV_LORA), F32)],
        compiler_params=_cparams("arbitrary"),
        name="sample_attention",
    )(page_table.reshape(-1), qa, qi, wi_col, qcat, dsa_new, mla_new, cache_dsa, cache_mla)


def _prep_mixer_weights(w_in_l, q_norm_l, w_uq_l, kv_norm_l, w_uk_l, w_uv_l, w_o_l):
    w1 = jnp.where(jnp.asarray(_SRC >= 0)[None, :], w_in_l[:, np.maximum(_SRC, 0)], 0.0)
    w1r = w1[:, _PARTNER[:N1P]] * jnp.asarray(_SIGN[:N1P])[None, :]
    uq = w_uq_l.reshape(Q_LORA, B_HEADS, QK_NOPE + QK_ROPE)
    uq_nope = uq[:, :, :QK_NOPE].reshape(Q_LORA, B_HEADS * QK_NOPE)
    uq_pe = uq[:, :, QK_NOPE:].reshape(Q_LORA, B_HEADS * QK_ROPE)
    uq_pe_r = uq_pe[:, _PARTNER[C_PE:] - C_PE] * jnp.asarray(_SIGN[C_PE:])[None, :]
    wuq = jnp.concatenate([uq_nope, uq_pe], axis=1)
    wukt = jnp.transpose(w_uk_l, (1, 2, 0))
    wuv = jnp.transpose(w_uv_l, (1, 0, 2))
    n_a = A_HEADS * A_HEAD_DIM
    proj = (w1.astype(BF16), w1r.astype(BF16), q_norm_l.reshape(1, -1), wuq.astype(BF16), uq_pe_r.astype(BF16),
            kv_norm_l.reshape(1, -1), wukt.astype(BF16))
    merge = (wuv.astype(BF16), w_o_l[:n_a].astype(BF16), w_o_l[n_a:].astype(BF16))
    return proj, merge


def _prep_ffn_weights(w_in_l, w_out_l):
    return w_in_l[:, :D_FF].astype(BF16), w_in_l[:, D_FF:].astype(BF16), w_out_l.astype(BF16)


def _token_tile(n, cap):
    best = 8
    for t in range(8, cap + 1, 8):
        if n % t == 0:
            best = t
    return best


PROMPT_TQ = 256
PROMPT_TM = 512


def kernel(x_prompt, x_sample, cache_dsa, cache_mla, page_table, meta_tokens, ln1_g, ln1_b, ffn1_w_in, ffn1_w_out,
           w_in, mla_q_norm, mla_w_uq, mla_kv_norm, mla_w_uk, mla_w_uv, w_o, ln2_g, ln2_b, ffn2_w_in, ffn2_w_out,
           ln3_g, ln3_b):
    nb, seq, _ = x_prompt.shape
    db, t_new, _ = x_sample.shape
    n_pages = page_table.shape[1]
    past = n_pages * PAGE_SIZE
    n_p = nb * seq
    n_s = db * t_new
    n_g = n_s + N_META
    tq = min(PROMPT_TQ, seq)
    tm_p = _token_tile(n_p, PROMPT_TM)
    tm_p = tm_p if seq % tm_p == 0 else _token_tile(seq, PROMPT_TM)
    tm_g = _token_tile(n_g, 1024)
    topk_p = min(TOPK_MAX, (seq + N_META) // 4)
    topk_s = min(TOPK_MAX, (past + t_new) // 4)

    freq = _rope_freqs()
    pos_p = (jnp.arange(seq, dtype=jnp.int32) + N_META).astype(F32).reshape(seq, 1)
    pos_g = jnp.concatenate([jnp.tile(past + jnp.arange(t_new, dtype=jnp.int32), db),
                             jnp.arange(N_META, dtype=jnp.int32)]).astype(F32).reshape(n_g, 1)
    cos_p, sin_p = _rope_table(pos_p, freq, _token_tile(seq, 256))
    cos_g, sin_g = _rope_table(pos_g, freq, _token_tile(n_g, 256))

    xp = x_prompt.reshape(n_p, D_MODEL)
    xg = jnp.concatenate([x_sample.reshape(n_s, D_MODEL), meta_tokens.astype(x_prompt.dtype)], axis=0)
    vec = lambda v: v.reshape(1, -1)
    rows_out = [[], [], [], []]
    for l in range(DEPTH):
        f1 = _prep_ffn_weights(ffn1_w_in[l], ffn1_w_out[l])
        f2 = _prep_ffn_weights(ffn2_w_in[l], ffn2_w_out[l])
        pw, mw = _prep_mixer_weights(w_in[l], mla_q_norm[l], mla_w_uq[l], mla_kv_norm[l], mla_w_uk[l],
                                     mla_w_uv[l], w_o[l])
        xp = _ffn(xp, *f1, vec(ln1_g[l]), vec(ln1_b[l]), tm_p)
        xg = _ffn(xg, *f1, vec(ln1_g[l]), vec(ln1_b[l]), tm_g)

        qa_p, qi_p, wi_p, qcat_p, dsa_p, mla_p = _project(xp, cos_p, sin_p, seq // tm_p, pw, tm_p)
        qa_g, qi_g, wi_g, qcat_g, dsa_g, mla_g = _project(xg, cos_g, sin_g, n_g // tm_g, pw, tm_g)

        zpad = jnp.zeros((LANE - N_META, ROW), F32)
        dsa_meta = jnp.concatenate([dsa_g[n_s:], zpad], axis=0)
        mla_meta = jnp.concatenate([mla_g[n_s:], zpad], axis=0)
        hm = lambda a: a[:, n_s:].reshape(8, 1, N_META, a.shape[-1])
        oa_m, olat_m = _prompt_attention(hm(qa_g), hm(qi_g), wi_g[n_s:].reshape(1, N_META, IDX_HEADS), hm(qcat_g),
                                         None, None, dsa_meta, mla_meta, blk=0, tq=N_META, q_pos0=0, top_k=topk_p)

        hp = lambda a: a.reshape(8, nb, seq, a.shape[-1])
        dsa_p3 = dsa_p.reshape(nb, seq, ROW)
        mla_p3 = mla_p.reshape(nb, seq, ROW)
        oa_blocks, olat_blocks = [], []
        for blk in range(seq // tq):
            oa_b, olat_b = _prompt_attention(hp(qa_p), hp(qi_p), wi_p.reshape(nb, seq, IDX_HEADS), hp(qcat_p),
                                             dsa_p3, mla_p3, dsa_meta, mla_meta, blk=blk, tq=tq, q_pos0=N_META,
                                             top_k=topk_p)
            oa_blocks.append(oa_b)
            olat_blocks.append(olat_b)
        oa_p = jnp.concatenate(oa_blocks, axis=1).reshape(n_p, -1)
        olat_p = jnp.concatenate(olat_blocks, axis=1).reshape(n_p, -1)

        hs = lambda a: a[:, :n_s].reshape(8, db, t_new, a.shape[-1])
        wi_col = jnp.transpose(wi_g[:n_s].reshape(db, t_new, IDX_HEADS), (0, 2, 1)).reshape(db, IDX_HEADS * t_new, 1)
        dsa_s = dsa_g[:n_s].reshape(db, t_new, ROW)
        mla_s = mla_g[:n_s].reshape(db, t_new, ROW)
        oa_s, olat_s = _sample_attention(page_table, hs(qa_g), hs(qi_g), wi_col, hs(qcat_g), dsa_s, mla_s,
                                         cache_dsa, cache_mla, layer=l, top_k=topk_s)
        oa_g = jnp.concatenate([oa_s.reshape(n_s, -1), oa_m.reshape(N_META, -1)], axis=0)
        olat_g = jnp.concatenate([olat_s.reshape(n_s, -1), olat_m.reshape(N_META, -1)], axis=0)

        xp = _merge(xp, oa_p, olat_p, *mw, vec(ln2_g[l]), vec(ln2_b[l]), tm_p)
        xg = _merge(xg, oa_g, olat_g, *mw, vec(ln2_g[l]), vec(ln2_b[l]), tm_g)
        xp = _ffn(xp, *f2, vec(ln3_g[l]), vec(ln3_b[l]), tm_p)
        xg = _ffn(xg, *f2, vec(ln3_g[l]), vec(ln3_b[l]), tm_g)

        bc = lambda m: jnp.broadcast_to(m[None, :N_META], (nb, N_META, ROW))
        rows_out[0].append(jnp.concatenate([bc(dsa_meta), dsa_p3], axis=1))
        rows_out[1].append(jnp.concatenate([bc(mla_meta), mla_p3], axis=1))
        rows_out[2].append(dsa_s)
        rows_out[3].append(mla_s)

    y_prompt = xp.reshape(nb, seq, D_MODEL)
    y_sample = xg[:n_s].reshape(db, t_new, D_MODEL)
    return (y_prompt, y_sample, jnp.stack(rows_out[0]), jnp.stack(rows_out[1]), jnp.stack(rows_out[2]),
            jnp.stack(rows_out[3]))
```

```python
import functools

import numpy as np
import jax
import jax.numpy as jnp
from jax import lax
from jax.experimental import pallas as pl
from jax.experimental.pallas import tpu as pltpu

D_MODEL = 1024
DEPTH = 2
PAGE_SIZE = 128
N_META = 16
A_HEADS = 8
A_HEAD_DIM = 64
A_ROT = 16
A_KV = 128
IDX_HEADS = 8
IDX_DIM = 32
IDX_ROT = 8
TOPK_MAX = 256
B_HEADS = 8
Q_LORA = 256
KV_LORA = 128
QK_NOPE = 64
QK_ROPE = 32
V_DIM = 64
D_FF = 2816
ROPE_THETA = 500000.0
DEEPNORM_ALPHA = (2 * DEPTH) ** 0.25
LN_EPS = 1e-5
RMS_EPS = 1e-6
ROW = 160

BF16 = jnp.bfloat16
F32 = jnp.float32
INT_MIN = -(2 ** 31)
NEG_INF = float("-inf")

VMEM_LIMIT = 56 * 1024 * 1024

C_AQ = 0
C_IQ = 512
C_CQ = 768
C_AKV = 1024
C_IK = 1152
C_CKV = 1280
C_KPE = 1408
C_IW = 1440
N1P = 1536
C_PE = N1P
N_TAB = N1P + B_HEADS * QK_ROPE

O_AQ, O_AKV, O_IQ, O_IK, O_IW, O_CQ, O_CKV, O_KPE = 0, 512, 640, 896, 928, 936, 1192, 1320


def _layout_tables():
    src = np.full((N1P,), -1, np.int64)
    for new, old, n in ((C_AQ, O_AQ, 512), (C_IQ, O_IQ, 256), (C_CQ, O_CQ, 256), (C_AKV, O_AKV, 128),
                        (C_IK, O_IK, 32), (C_CKV, O_CKV, 128), (C_KPE, O_KPE, 32), (C_IW, O_IW, 8)):
        src[new:new + n] = np.arange(old, old + n)
    partner = np.arange(N_TAB)
    sign = np.zeros((N_TAB,), np.float32)
    fidx = np.full((N_TAB,), -1, np.int64)
    half_of = np.ones((N_TAB,), np.int64)
    for start, heads, hd, rot in ((C_AQ, A_HEADS, A_HEAD_DIM, A_ROT), (C_AKV, 1, A_HEAD_DIM, A_ROT),
                                  (C_IQ, IDX_HEADS, IDX_DIM, IDX_ROT), (C_IK, 1, IDX_DIM, IDX_ROT),
                                  (C_KPE, 1, QK_ROPE, QK_ROPE), (C_PE, B_HEADS, QK_ROPE, QK_ROPE)):
        half = rot // 2
        for h in range(heads):
            for d in range(rot):
                c = start + h * hd + d
                if d < half:
                    partner[c], sign[c] = c + half, -1.0
                else:
                    partner[c], sign[c] = c - half, 1.0
                fidx[c] = d % half
                half_of[c] = half
    return src, partner, sign, fidx, half_of


_SRC, _PARTNER, _SIGN, _FIDX, _HALF = _layout_tables()


def _rope_freqs():
    f = jnp.zeros((N_TAB,), F32)
    for half in (A_ROT // 2, IDX_ROT // 2, QK_ROPE // 2):
        inv = ROPE_THETA ** (-jnp.arange(half, dtype=F32) / half)
        sel = (_HALF == half) & (_FIDX >= 0)
        cols = np.nonzero(sel)[0]
        f = f.at[cols].set(inv[_FIDX[cols]])
    return f.reshape(1, N_TAB)


def _cparams(*sem):
    return pltpu.CompilerParams(dimension_semantics=sem, vmem_limit_bytes=VMEM_LIMIT)


def _resident(shape):
    nd = len(shape)
    return pl.BlockSpec(shape, lambda *_: (0,) * nd, pipeline_mode=pl.Buffered(1))


def _layernorm(y, g, b):
    mu = jnp.mean(y, axis=-1, keepdims=True)
    yc = y - mu
    var = jnp.mean(yc * yc, axis=-1, keepdims=True)
    return yc * lax.rsqrt(var + LN_EPS) * g + b


def _rmsnorm(y, g):
    return y * lax.rsqrt(jnp.mean(y * y, axis=-1, keepdims=True) + RMS_EPS) * g


def _dot(a, b):
    return jnp.dot(a, b, preferred_element_type=F32)


def _dot_nt(a, b):
    return lax.dot_general(a, b, (((1,), (1,)), ((), ())), preferred_element_type=F32)


def _rope_table_kernel(pos_ref, freq_ref, cos_ref, sin_ref):
    ang = pos_ref[...] * freq_ref[...]
    cos_ref[...] = jnp.cos(ang)
    sin_ref[...] = jnp.sin(ang)


def _rope_table(pos, freq, tr):
    r, n = pos.shape[0], freq.shape[1]
    return pl.pallas_call(
        _rope_table_kernel,
        grid=(r // tr,),
        in_specs=[pl.BlockSpec((tr, 1), lambda i: (i, 0)), pl.BlockSpec((1, n), lambda i: (0, 0))],
        out_specs=[pl.BlockSpec((tr, n), lambda i: (i, 0))] * 2,
        out_shape=[jax.ShapeDtypeStruct((r, n), F32)] * 2,
        compiler_params=_cparams("parallel"),
        name="rope_table",
    )(pos, freq)


FF_CHUNK = 256


def _ffn_kernel(x_ref, wg_ref, wu_ref, wo_ref, g_ref, b_ref, o_ref, acc_ref):
    x = x_ref[...]
    xb = x.astype(BF16)
    for c in range(D_FF // FF_CHUNK):
        sl = slice(c * FF_CHUNK, (c + 1) * FF_CHUNK)
        gate = _dot(xb, wg_ref[:, sl])
        up = _dot(xb, wu_ref[:, sl])
        hid = (gate * jax.nn.sigmoid(gate) * up).astype(BF16)
        part = _dot(hid, wo_ref[sl, :])
        if c == 0:
            acc_ref[...] = part
        else:
            acc_ref[...] += part
    y = DEEPNORM_ALPHA * x + 0.5 * acc_ref[...]
    o_ref[...] = _layernorm(y, g_ref[...], b_ref[...])


def _ffn(x, wg, wu, wo, g, b, tm):
    n = x.shape[0]
    row = pl.BlockSpec((tm, D_MODEL), lambda i: (i, 0))
    return pl.pallas_call(
        _ffn_kernel,
        grid=(n // tm,),
        in_specs=[row, _resident(wg.shape), _resident(wu.shape), _resident(wo.shape),
                  _resident(g.shape), _resident(b.shape)],
        out_specs=row,
        out_shape=jax.ShapeDtypeStruct((n, D_MODEL), F32),
        scratch_shapes=[pltpu.VMEM((tm, D_MODEL), F32)],
        compiler_params=_cparams("parallel"),
        name="ffn",
    )(x, wg, wu, wo, g, b)


def _proj_kernel(x_ref, cos_ref, sin_ref, w1_ref, w1r_ref, qn_ref, wuq_ref, wuqr_ref, kvn_ref, wuk_ref,
                 qa_ref, qi_ref, wi_ref, qcat_ref, dsa_ref, mla_ref):
    xb = x_ref[...].astype(BF16)
    cos = cos_ref[...]
    sin = sin_ref[...]
    z = _dot(xb, w1_ref[...]) * cos[:, :N1P] + _dot(xb, w1r_ref[...]) * sin[:, :N1P]
    for h in range(A_HEADS):
        qa_ref[h] = z[:, C_AQ + h * A_HEAD_DIM:C_AQ + (h + 1) * A_HEAD_DIM]
    for h in range(IDX_HEADS):
        qi_ref[h] = z[:, C_IQ + h * IDX_DIM:C_IQ + (h + 1) * IDX_DIM]
    wi_ref[...] = z[:, C_IW:C_IW + IDX_HEADS] * ((IDX_HEADS * IDX_DIM) ** -0.5)
    dsa_ref[...] = z[:, C_AKV:C_AKV + ROW]
    mla_ref[:, :KV_LORA] = _rmsnorm(z[:, C_CKV:C_CKV + KV_LORA], kvn_ref[...])
    mla_ref[:, KV_LORA:] = z[:, C_KPE:C_KPE + QK_ROPE]
    cq = _rmsnorm(z[:, C_CQ:C_CQ + Q_LORA], qn_ref[...]).astype(BF16)
    qb = _dot(cq, wuq_ref[...])
    n_nope = B_HEADS * QK_NOPE
    q_pe = qb[:, n_nope:] * cos[:, C_PE:] + _dot(cq, wuqr_ref[...]) * sin[:, C_PE:]
    for h in range(B_HEADS):
        q_nope = qb[:, h * QK_NOPE:(h + 1) * QK_NOPE].astype(BF16)
        qcat_ref[h, :, :KV_LORA] = _dot(q_nope, wuk_ref[h])
        qcat_ref[h, :, KV_LORA:] = q_pe[:, h * QK_ROPE:(h + 1) * QK_ROPE]


def _project(x, cos, sin, tab_blocks, w, tm):
    n = x.shape[0]
    w1, w1r, qn, wuq, wuqr, kvn, wukt = w
    row = lambda width: pl.BlockSpec((tm, width), lambda i: (i, 0))
    heads = lambda width: pl.BlockSpec((8, tm, width), lambda i: (0, i, 0))
    tab = pl.BlockSpec((tm, N_TAB), lambda i: (i % tab_blocks, 0))
    return pl.pallas_call(
        _proj_kernel,
        grid=(n // tm,),
        in_specs=[row(D_MODEL), tab, tab, _resident(w1.shape), _resident(w1r.shape), _resident(qn.shape),
                  _resident(wuq.shape), _resident(wuqr.shape), _resident(kvn.shape), _resident(wukt.shape)],
        out_specs=[heads(A_HEAD_DIM), heads(IDX_DIM), row(IDX_HEADS), heads(ROW), row(ROW), row(ROW)],
        out_shape=[jax.ShapeDtypeStruct((A_HEADS, n, A_HEAD_DIM), F32),
                   jax.ShapeDtypeStruct((IDX_HEADS, n, IDX_DIM), F32),
                   jax.ShapeDtypeStruct((n, IDX_HEADS), F32),
                   jax.ShapeDtypeStruct((B_HEADS, n, ROW), F32),
                   jax.ShapeDtypeStruct((n, ROW), F32),
                   jax.ShapeDtypeStruct((n, ROW), F32)],
        compiler_params=_cparams("parallel"),
        name="project",
    )(x, cos, sin, w1, w1r, qn, wuq, wuqr, kvn, wukt)


def _merge_kernel(x_ref, oa_ref, olat_ref, wuv_ref, woa_ref, wob_ref, g_ref, b_ref, o_ref):
    ob = [_dot(olat_ref[:, h * KV_LORA:(h + 1) * KV_LORA].astype(BF16), wuv_ref[h]) for h in range(B_HEADS)]
    ob = jnp.concatenate(ob, axis=-1).astype(BF16)
    mix = _dot(oa_ref[...].astype(BF16), woa_ref[...]) + _dot(ob, wob_ref[...])
    o_ref[...] = _layernorm(DEEPNORM_ALPHA * x_ref[...] + mix, g_ref[...], b_ref[...])


def _merge(x, oa, olat, wuv, woa, wob, g, b, tm):
    n = x.shape[0]
    row = lambda width: pl.BlockSpec((tm, width), lambda i: (i, 0))
    return pl.pallas_call(
        _merge_kernel,
        grid=(n // tm,),
        in_specs=[row(D_MODEL), row(A_HEADS * A_HEAD_DIM), row(B_HEADS * KV_LORA), _resident(wuv.shape),
                  _resident(woa.shape), _resident(wob.shape), _resident(g.shape), _resident(b.shape)],
        out_specs=row(D_MODEL),
        out_shape=jax.ShapeDtypeStruct((n, D_MODEL), F32),
        compiler_params=_cparams("parallel"),
        name="merge",
    )(x, oa, olat, wuv, woa, wob, g, b)


LANE = 128
SELECT_BITS = 32


def _order_key(score):
    bits = lax.bitcast_convert_type(score + 0.0, jnp.int32)
    return bits ^ ((bits >> 31) & jnp.int32(0x7FFFFFFF))


def _count_ge(keys, cand):
    return jnp.sum(jnp.where(keys >= cand, 1.0, 0.0), axis=1, keepdims=True)


def _kth_largest_key(key_ref, top_k, unroll):
    rows = key_ref.shape[0]

    def body(it, t):
        bit = lax.shift_left(jnp.int32(1), jnp.int32(SELECT_BITS - 1) - it)
        cand = t ^ bit
        return jnp.where(_count_ge(key_ref[...], cand) >= top_k, cand, t)

    t = jnp.full((rows, 1), INT_MIN, jnp.int32)
    if unroll:
        for hi in range(SELECT_BITS - 1, 0, -2):
            keys = key_ref[...]
            b_hi = jnp.int32(INT_MIN if hi == 31 else 1 << hi)
            b_lo = jnp.int32(1 << (hi - 1))
            c1, c2 = t ^ b_lo, t ^ b_hi
            c3 = c2 ^ b_lo
            n1, n2, n3 = _count_ge(keys, c1), _count_ge(keys, c2), _count_ge(keys, c3)
            t = jnp.where(n3 >= top_k, c3, jnp.where(n2 >= top_k, c2, jnp.where(n1 >= top_k, c1, t)))
        return t
    return lax.fori_loop(0, SELECT_BITS, body, t)


def _select_bias(key_ref, bias_ref, top_k, lane_order, unroll=False):
    kth = jnp.maximum(_kth_largest_key(key_ref, top_k, unroll), jnp.int32(INT_MIN + 1))
    keys = key_ref[...]
    bias_ref[...] = jnp.where(keys >= kth, 0.0, NEG_INF)
    n_ge = _count_ge(keys, kth)

    @pl.when(jnp.max(n_ge) > top_k)
    def _():
        ks = key_ref[...]
        need = top_k - jnp.sum(jnp.where(ks > kth, 1.0, 0.0), axis=1, keepdims=True)
        r_i = lax.broadcasted_iota(jnp.int32, (LANE, LANE), 0)
        c_i = lax.broadcasted_iota(jnp.int32, (LANE, LANE), 1)
        tri = jnp.where(r_i <= c_i, 1.0, 0.0).astype(BF16)
        seen = jnp.zeros_like(need)
        for blk in lane_order:
            sl = slice(blk * LANE, (blk + 1) * LANE)
            kb = key_ref[:, sl]
            tie = kb == kth
            tie_f = jnp.where(tie, 1.0, 0.0)
            rank = _dot(tie_f.astype(BF16), tri) + seen
            keep = (kb > kth) | (tie & (rank <= need))
            bias_ref[:, sl] = jnp.where(keep, 0.0, NEG_INF)
            seen = seen + jnp.sum(tie_f, axis=1, keepdims=True)


def _softmax_rows(s):
    m = jnp.max(s, axis=-1, keepdims=True)
    p = jnp.exp(s - m)
    return p, 1.0 / jnp.sum(p, axis=-1, keepdims=True)


def _prompt_positions(tq, n_real, q_pos0):
    lk = n_real + LANE
    lane = lax.broadcasted_iota(jnp.int32, (1, lk), 1)
    big = jnp.int32(2 ** 30)
    key_pos = jnp.where(lane < n_real, lane + N_META, jnp.where(lane < n_real + N_META, lane - n_real, big))
    q_pos = lax.broadcasted_iota(jnp.int32, (tq, 1), 0) + q_pos0
    return key_pos <= q_pos


def _gather_keys(real_ref, meta_ref, n_real):
    if n_real == 0:
        return meta_ref[...].astype(BF16)
    return jnp.concatenate([real_ref[...], meta_ref[...]], axis=0).astype(BF16)


def _dsa_prompt_kernel(qa_ref, qi_ref, wi_ref, real_ref, meta_ref, o_ref, key_ref, bias_ref, *,
                       tq, n_real, q_pos0, top_k):
    kb = _gather_keys(real_ref, meta_ref, n_real)
    k_idx = kb[:, A_KV:]
    wi = wi_ref[...]
    score = None
    for h in range(IDX_HEADS):
        d = jnp.maximum(_dot_nt(qi_ref[h].astype(BF16), k_idx), 0.0) * wi[:, h:h + 1]
        score = d if score is None else score + d
    adm = _prompt_positions(tq, n_real, q_pos0)
    key_ref[...] = jnp.where(adm, _order_key(score), jnp.int32(INT_MIN))
    n_blk = n_real // LANE
    _select_bias(key_ref, bias_ref, top_k, [n_blk] + list(range(n_blk)))
    k_att = kb[:, :A_HEAD_DIM]
    v_att = kb[:, A_HEAD_DIM:A_KV]
    for h in range(A_HEADS):
        s = _dot_nt((qa_ref[h] * (A_HEAD_DIM ** -0.5)).astype(BF16), k_att) + bias_ref[...]
        p, inv = _softmax_rows(s)
        o_ref[:, h * A_HEAD_DIM:(h + 1) * A_HEAD_DIM] = _dot(p.astype(BF16), v_att) * inv


def _mla_prompt_kernel(q_ref, real_ref, meta_ref, o_ref, *, tq, n_real, q_pos0):
    kb = _gather_keys(real_ref, meta_ref, n_real)
    c_kv = kb[:, :KV_LORA]
    bias = jnp.where(_prompt_positions(tq, n_real, q_pos0), 0.0, NEG_INF)
    scale = (QK_NOPE + QK_ROPE) ** -0.5
    for h in range(B_HEADS):
        s = _dot_nt(q_ref[h].astype(BF16), kb) * scale + bias
        p, inv = _softmax_rows(s)
        o_ref[:, h * KV_LORA:(h + 1) * KV_LORA] = _dot(p.astype(BF16), c_kv) * inv


def _prompt_attention(qa, qi, wi, qcat, dsa_real, mla_real, dsa_meta, mla_meta, *, blk, tq, q_pos0, top_k):
    nb = qa.shape[1]
    has_real = dsa_real is not None
    n_real = (blk + 1) * tq if has_real else 0
    lk = n_real + LANE
    if not has_real:
        dsa_real = jnp.zeros((nb, 8, ROW), F32)
        mla_real = dsa_real
    nr_blk = max(n_real, 8)
    heads = lambda width: pl.BlockSpec((8, None, tq, width), lambda b: (0, b, blk, 0))
    rows = lambda width: pl.BlockSpec((None, tq, width), lambda b: (b, blk, 0))
    real = pl.BlockSpec((None, nr_blk, ROW), lambda b: (b, 0, 0))
    meta = _resident((LANE, ROW))
    pos0 = q_pos0 + blk * tq
    o_a = pl.pallas_call(
        functools.partial(_dsa_prompt_kernel, tq=tq, n_real=n_real, q_pos0=pos0, top_k=top_k),
        grid=(nb,),
        in_specs=[heads(A_HEAD_DIM), heads(IDX_DIM), rows(IDX_HEADS), real, meta],
        out_specs=pl.BlockSpec((None, tq, A_HEADS * A_HEAD_DIM), lambda b: (b, 0, 0)),
        out_shape=jax.ShapeDtypeStruct((nb, tq, A_HEADS * A_HEAD_DIM), F32),
        scratch_shapes=[pltpu.VMEM((tq, lk), jnp.int32), pltpu.VMEM((tq, lk), F32)],
        compiler_params=_cparams("parallel"),
        name=f"dsa_prompt_{n_real}",
    )(qa, qi, wi, dsa_real, dsa_meta)
    o_lat = pl.pallas_call(
        functools.partial(_mla_prompt_kernel, tq=tq, n_real=n_real, q_pos0=pos0),
        grid=(nb,),
        in_specs=[heads(ROW), real, meta],
        out_specs=pl.BlockSpec((None, tq, B_HEADS * KV_LORA), lambda b: (b, 0, 0)),
        out_shape=jax.ShapeDtypeStruct((nb, tq, B_HEADS * KV_LORA), F32),
        compiler_params=_cparams("parallel"),
        name=f"mla_prompt_{n_real}",
    )(qcat, mla_real, mla_meta)
    return o_a, o_lat


def _sample_kernel(pt_ref, qa_ref, qi_ref, wi_ref, qcat_ref, dsa_new_ref, mla_new_ref, dsa_hbm, mla_hbm,
                   oa_ref, olat_ref, dsa_buf, mla_buf, sems, key_ref, bias_ref, *, layer, n_pages, t_new, top_k):
    b = pl.program_id(0)
    nb = pl.num_programs(0)
    slot = b % 2
    past = n_pages * PAGE_SIZE
    lk = past + PAGE_SIZE

    def page_copies(seq, slt):
        out = []
        for p in range(n_pages):
            page = pt_ref[seq * n_pages + p]
            window = pl.ds(p * PAGE_SIZE, PAGE_SIZE)
            out.append(pltpu.make_async_copy(dsa_hbm.at[layer, page], dsa_buf.at[slt, :, window], sems.at[0, slt]))
            out.append(pltpu.make_async_copy(mla_hbm.at[layer, page], mla_buf.at[slt, :, window], sems.at[1, slt]))
        return out

    @pl.when(b == 0)
    def _():
        for c in page_copies(b, slot):
            c.start()

    @pl.when(b + 1 < nb)
    def _():
        for c in page_copies(b + 1, 1 - slot):
            c.start()

    dsa_buf[slot, :, past:] = dsa_new_ref[...]
    mla_buf[slot, :, past:] = mla_new_ref[...]

    for c in page_copies(b, slot):
        c.wait()

    n_row = A_HEADS * t_new
    lane = lax.broadcasted_iota(jnp.int32, (1, lk), 1)
    q_id = lax.broadcasted_iota(jnp.int32, (t_new, 1), 0)
    adm = lane <= q_id + past

    def per_head(bias):
        return jnp.concatenate([bias] * A_HEADS, axis=0)

    idx_t = dsa_buf[slot, A_KV:, :].astype(BF16)
    d = jnp.maximum(_dot(qi_ref[...].reshape(n_row, IDX_DIM).astype(BF16), idx_t), 0.0) * wi_ref[...]
    score = d[0:t_new]
    for h in range(1, IDX_HEADS):
        score = score + d[h * t_new:(h + 1) * t_new]
    key_ref[...] = jnp.where(adm, _order_key(score), jnp.int32(INT_MIN))

    km = mla_buf[slot].astype(BF16)
    s = _dot(qcat_ref[...].reshape(n_row, ROW).astype(BF16), km) * ((QK_NOPE + QK_ROPE) ** -0.5)
    p, inv = _softmax_rows(s + per_head(jnp.where(adm, 0.0, NEG_INF)))
    o = _dot_nt(p.astype(BF16), km[:KV_LORA]) * inv
    for h in range(B_HEADS):
        olat_ref[:, h * KV_LORA:(h + 1) * KV_LORA] = o[h * t_new:(h + 1) * t_new]

    _select_bias(key_ref, bias_ref, top_k, list(range(lk // LANE)), unroll=True)

    k_t = dsa_buf[slot, :A_HEAD_DIM, :].astype(BF16)
    v_t = dsa_buf[slot, A_HEAD_DIM:A_KV, :].astype(BF16)
    qa = (qa_ref[...].reshape(n_row, A_HEAD_DIM) * (A_HEAD_DIM ** -0.5)).astype(BF16)
    p, inv = _softmax_rows(_dot(qa, k_t) + per_head(bias_ref[...]))
    o = _dot_nt(p.astype(BF16), v_t) * inv
    for h in range(A_HEADS):
        oa_ref[:, h * A_HEAD_DIM:(h + 1) * A_HEAD_DIM] = o[h * t_new:(h + 1) * t_new]


def _sample_attention(page_table, qa, qi, wi_col, qcat, dsa_new_t, mla_new_t, cache_dsa_t, cache_mla_t, *, layer,
                      top_k):
    nb, n_pages = page_table.shape
    t_new = qa.shape[2]
    lk = (n_pages + 1) * PAGE_SIZE
    heads = lambda width: pl.BlockSpec((8, None, t_new, width), lambda b, pt: (0, b, 0, 0))
    rows = lambda n, width: pl.BlockSpec((None, n, width), lambda b, pt: (b, 0, 0))
    grid_spec = pltpu.PrefetchScalarGridSpec(
        num_scalar_prefetch=1,
        grid=(nb,),
        in_specs=[heads(A_HEAD_DIM), heads(IDX_DIM), rows(8 * t_new, 1), heads(ROW), rows(ROW, PAGE_SIZE),
                  rows(ROW, PAGE_SIZE), pl.BlockSpec(memory_space=pl.ANY), pl.BlockSpec(memory_space=pl.ANY)],
        out_specs=[rows(t_new, A_HEADS * A_HEAD_DIM), rows(t_new, B_HEADS * KV_LORA)],
        scratch_shapes=[pltpu.VMEM((2, ROW, lk), F32),
                        pltpu.VMEM((2, ROW, lk), F32),
                        pltpu.SemaphoreType.DMA((2, 2)),
                        pltpu.VMEM((t_new, lk), jnp.int32),
                        pltpu.VMEM((t_new, lk), F32)],
    )
    return pl.pallas_call(
        functools.partial(_sample_kernel, layer=layer, n_pages=n_pages, t_new=t_new, top_k=top_k),
        grid_spec=grid_spec,
        out_shape=[jax.ShapeDtypeStruct((nb, t_new, A_HEADS * A_HEAD_DIM), F32),
                   jax.ShapeDtypeStruct((nb, t_new, B_HEADS * KV_LORA), F32)],
        compiler_params=_cparams("arbitrary"),
        name="sample_attention",
    )(page_table.reshape(-1), qa, qi, wi_col, qcat, dsa_new_t, mla_new_t, cache_dsa_t, cache_mla_t)


def _prep_mixer_weights(w_in_l, q_norm_l, w_uq_l, kv_norm_l, w_uk_l, w_uv_l, w_o_l):
    w1 = jnp.where(jnp.asarray(_SRC >= 0)[None, :], w_in_l[:, np.maximum(_SRC, 0)], 0.0)
    w1r = w1[:, _PARTNER[:N1P]] * jnp.asarray(_SIGN[:N1P])[None, :]
    uq = w_uq_l.reshape(Q_LORA, B_HEADS, QK_NOPE + QK_ROPE)
    uq_nope = uq[:, :, :QK_NOPE].reshape(Q_LORA, B_HEADS * QK_NOPE)
    uq_pe = uq[:, :, QK_NOPE:].reshape(Q_LORA, B_HEADS * QK_ROPE)
    uq_pe_r = uq_pe[:, _PARTNER[C_PE:] - C_PE] * jnp.asarray(_SIGN[C_PE:])[None, :]
    wuq = jnp.concatenate([uq_nope, uq_pe], axis=1)
    wukt = jnp.transpose(w_uk_l, (1, 2, 0))
    wuv = jnp.transpose(w_uv_l, (1, 0, 2))
    n_a = A_HEADS * A_HEAD_DIM
    proj = (w1.astype(BF16), w1r.astype(BF16), q_norm_l.reshape(1, -1), wuq.astype(BF16), uq_pe_r.astype(BF16),
            kv_norm_l.reshape(1, -1), wukt.astype(BF16))
    merge = (wuv.astype(BF16), w_o_l[:n_a].astype(BF16), w_o_l[n_a:].astype(BF16))
    return proj, merge


def _prep_ffn_weights(w_in_l, w_out_l):
    return w_in_l[:, :D_FF].astype(BF16), w_in_l[:, D_FF:].astype(BF16), w_out_l.astype(BF16)


def _token_tile(n, cap):
    best = 8
    for t in range(8, cap + 1, 8):
        if n % t == 0:
            best = t
    return best


PROMPT_TQ = 256
PROMPT_TM = 512


def kernel(x_prompt, x_sample, cache_dsa, cache_mla, page_table, meta_tokens, ln1_g, ln1_b, ffn1_w_in, ffn1_w_out,
           w_in, mla_q_norm, mla_w_uq, mla_kv_norm, mla_w_uk, mla_w_uv, w_o, ln2_g, ln2_b, ffn2_w_in, ffn2_w_out,
           ln3_g, ln3_b):
    nb, seq, _ = x_prompt.shape
    db, t_new, _ = x_sample.shape
    n_pages = page_table.shape[1]
    past = n_pages * PAGE_SIZE
    n_p = nb * seq
    n_s = db * t_new
    n_g = n_s + N_META
    tq = min(PROMPT_TQ, seq)
    tm_p = _token_tile(n_p, PROMPT_TM)
    tm_p = tm_p if seq % tm_p == 0 else _token_tile(seq, PROMPT_TM)
    tm_g = _token_tile(n_g, 1024)
    topk_p = min(TOPK_MAX, (seq + N_META) // 4)
    topk_s = min(TOPK_MAX, (past + t_new) // 4)

    freq = _rope_freqs()
    pos_p = (jnp.arange(seq, dtype=jnp.int32) + N_META).astype(F32).reshape(seq, 1)
    pos_g = jnp.concatenate([jnp.tile(past + jnp.arange(t_new, dtype=jnp.int32), db),
                             jnp.arange(N_META, dtype=jnp.int32)]).astype(F32).reshape(n_g, 1)
    cos_p, sin_p = _rope_table(pos_p, freq, _token_tile(seq, 256))
    cos_g, sin_g = _rope_table(pos_g, freq, _token_tile(n_g, 256))

    cache_dsa_t = jnp.swapaxes(cache_dsa, 2, 3)
    cache_mla_t = jnp.swapaxes(cache_mla, 2, 3)

    xp = x_prompt.reshape(n_p, D_MODEL)
    xg = jnp.concatenate([x_sample.reshape(n_s, D_MODEL), meta_tokens.astype(x_prompt.dtype)], axis=0)
    vec = lambda v: v.reshape(1, -1)
    rows_out = [[], [], [], []]
    for l in range(DEPTH):
        f1 = _prep_ffn_weights(ffn1_w_in[l], ffn1_w_out[l])
        f2 = _prep_ffn_weights(ffn2_w_in[l], ffn2_w_out[l])
        pw, mw = _prep_mixer_weights(w_in[l], mla_q_norm[l], mla_w_uq[l], mla_kv_norm[l], mla_w_uk[l],
                                     mla_w_uv[l], w_o[l])
        xp = _ffn(xp, *f1, vec(ln1_g[l]), vec(ln1_b[l]), tm_p)
        xg = _ffn(xg, *f1, vec(ln1_g[l]), vec(ln1_b[l]), tm_g)

        qa_p, qi_p, wi_p, qcat_p, dsa_p, mla_p = _project(xp, cos_p, sin_p, seq // tm_p, pw, tm_p)
        qa_g, qi_g, wi_g, qcat_g, dsa_g, mla_g = _project(xg, cos_g, sin_g, n_g // tm_g, pw, tm_g)

        zpad = jnp.zeros((LANE - N_META, ROW), F32)
        dsa_meta = jnp.concatenate([dsa_g[n_s:], zpad], axis=0)
        mla_meta = jnp.concatenate([mla_g[n_s:], zpad], axis=0)
        hm = lambda a: a[:, n_s:].reshape(8, 1, N_META, a.shape[-1])
        oa_m, olat_m = _prompt_attention(hm(qa_g), hm(qi_g), wi_g[n_s:].reshape(1, N_META, IDX_HEADS), hm(qcat_g),
                                         None, None, dsa_meta, mla_meta, blk=0, tq=N_META, q_pos0=0, top_k=topk_p)

        hp = lambda a: a.reshape(8, nb, seq, a.shape[-1])
        dsa_p3 = dsa_p.reshape(nb, seq, ROW)
        mla_p3 = mla_p.reshape(nb, seq, ROW)
        oa_blocks, olat_blocks = [], []
        for blk in range(seq // tq):
            oa_b, olat_b = _prompt_attention(hp(qa_p), hp(qi_p), wi_p.reshape(nb, seq, IDX_HEADS), hp(qcat_p),
                                             dsa_p3, mla_p3, dsa_meta, mla_meta, blk=blk, tq=tq, q_pos0=N_META,
                                             top_k=topk_p)
            oa_blocks.append(oa_b)
            olat_blocks.append(olat_b)
        oa_p = jnp.concatenate(oa_blocks, axis=1).reshape(n_p, -1)
        olat_p = jnp.concatenate(olat_blocks, axis=1).reshape(n_p, -1)

        hs = lambda a: a[:, :n_s].reshape(8, db, t_new, a.shape[-1])
        wi_col = jnp.transpose(wi_g[:n_s].reshape(db, t_new, IDX_HEADS), (0, 2, 1)).reshape(db, IDX_HEADS * t_new, 1)
        dsa_s = dsa_g[:n_s].reshape(db, t_new, ROW)
        mla_s = mla_g[:n_s].reshape(db, t_new, ROW)
        new_t = lambda r: jnp.pad(jnp.swapaxes(r, 1, 2), ((0, 0), (0, 0), (0, PAGE_SIZE - t_new)))
        oa_s, olat_s = _sample_attention(page_table, hs(qa_g), hs(qi_g), wi_col, hs(qcat_g), new_t(dsa_s),
                                         new_t(mla_s), cache_dsa_t, cache_mla_t, layer=l, top_k=topk_s)
        oa_g = jnp.concatenate([oa_s.reshape(n_s, -1), oa_m.reshape(N_META, -1)], axis=0)
        olat_g = jnp.concatenate([olat_s.reshape(n_s, -1), olat_m.reshape(N_META, -1)], axis=0)

        xp = _merge(xp, oa_p, olat_p, *mw, vec(ln2_g[l]), vec(ln2_b[l]), tm_p)
        xg = _merge(xg, oa_g, olat_g, *mw, vec(ln2_g[l]), vec(ln2_b[l]), tm_g)
        xp = _ffn(xp, *f2, vec(ln3_g[l]), vec(ln3_b[l]), tm_p)
        xg = _ffn(xg, *f2, vec(ln3_g[l]), vec(ln3_b[l]), tm_g)

        bc = lambda m: jnp.broadcast_to(m[None, :N_META], (nb, N_META, ROW))
        rows_out[0].append(jnp.concatenate([bc(dsa_meta), dsa_p3], axis=1))
        rows_out[1].append(jnp.concatenate([bc(mla_meta), mla_p3], axis=1))
        rows_out[2].append(dsa_s)
        rows_out[3].append(mla_s)

    y_prompt = xp.reshape(nb, seq, D_MODEL)
    y_sample = xg[:n_s].reshape(db, t_new, D_MODEL)
    return (y_prompt, y_sample, jnp.stack(rows_out[0]), jnp.stack(rows_out[1]), jnp.stack(rows_out[2]),
            jnp.stack(rows_out[3]))
```

```python
import functools

import numpy as np
import jax
import jax.numpy as jnp
from jax import lax
from jax.experimental import pallas as pl
from jax.experimental.pallas import tpu as pltpu

D_MODEL = 1024
DEPTH = 2
PAGE_SIZE = 128
N_META = 16
A_HEADS = 8
A_HEAD_DIM = 64
A_ROT = 16
A_KV = 128
IDX_HEADS = 8
IDX_DIM = 32
IDX_ROT = 8
TOPK_MAX = 256
B_HEADS = 8
Q_LORA = 256
KV_LORA = 128
QK_NOPE = 64
QK_ROPE = 32
V_DIM = 64
D_FF = 2816
ROPE_THETA = 500000.0
DEEPNORM_ALPHA = (2 * DEPTH) ** 0.25
LN_EPS = 1e-5
RMS_EPS = 1e-6
ROW = 160

BF16 = jnp.bfloat16
F32 = jnp.float32
INT_MIN = -(2 ** 31)
NEG_INF = float("-inf")

VMEM_LIMIT = 56 * 1024 * 1024

C_AQ = 0
C_IQ = 512
C_CQ = 768
C_AKV = 1024
C_IK = 1152
C_CKV = 1280
C_KPE = 1408
C_IW = 1440
N1P = 1536
C_PE = N1P
N_TAB = N1P + B_HEADS * QK_ROPE

O_AQ, O_AKV, O_IQ, O_IK, O_IW, O_CQ, O_CKV, O_KPE = 0, 512, 640, 896, 928, 936, 1192, 1320


def _layout_tables():
    fidx = np.full((N_TAB,), -1, np.int64)
    half_of = np.ones((N_TAB,), np.int64)
    for start, heads, hd, rot in ((C_AQ, A_HEADS, A_HEAD_DIM, A_ROT), (C_AKV, 1, A_HEAD_DIM, A_ROT),
                                  (C_IQ, IDX_HEADS, IDX_DIM, IDX_ROT), (C_IK, 1, IDX_DIM, IDX_ROT),
                                  (C_KPE, 1, QK_ROPE, QK_ROPE), (C_PE, B_HEADS, QK_ROPE, QK_ROPE)):
        half = rot // 2
        for h in range(heads):
            for d in range(rot):
                c = start + h * hd + d
                fidx[c] = d % half
                half_of[c] = half
    return fidx, half_of


_FIDX, _HALF = _layout_tables()


def _rope_freqs():
    f = jnp.zeros((N_TAB,), F32)
    for half in (A_ROT // 2, IDX_ROT // 2, QK_ROPE // 2):
        inv = ROPE_THETA ** (-jnp.arange(half, dtype=F32) / half)
        sel = (_HALF == half) & (_FIDX >= 0)
        cols = np.nonzero(sel)[0]
        f = f.at[cols].set(inv[_FIDX[cols]])
    return f.reshape(1, N_TAB)


def _cparams(*sem):
    return pltpu.CompilerParams(dimension_semantics=sem, vmem_limit_bytes=VMEM_LIMIT)


def _resident(shape):
    nd = len(shape)
    return pl.BlockSpec(shape, lambda *_: (0,) * nd, pipeline_mode=pl.Buffered(1))


def _layernorm(y, g, b):
    mu = jnp.mean(y, axis=-1, keepdims=True)
    yc = y - mu
    var = jnp.mean(yc * yc, axis=-1, keepdims=True)
    return yc * lax.rsqrt(var + LN_EPS) * g + b


def _rmsnorm(y, g):
    return y * lax.rsqrt(jnp.mean(y * y, axis=-1, keepdims=True) + RMS_EPS) * g


def _dot(a, b):
    return jnp.dot(a, b, preferred_element_type=F32)


def _dot_nt(a, b):
    return lax.dot_general(a, b, (((1,), (1,)), ((), ())), preferred_element_type=F32)


def _rope_table_kernel(pos_ref, freq_ref, cos_ref, sin_ref):
    ang = pos_ref[...] * freq_ref[...]
    cos_ref[...] = jnp.cos(ang)
    sin_ref[...] = jnp.sin(ang)


def _rope_table(pos, freq, tr):
    r, n = pos.shape[0], freq.shape[1]
    return pl.pallas_call(
        _rope_table_kernel,
        grid=(r // tr,),
        in_specs=[pl.BlockSpec((tr, 1), lambda i: (i, 0)), pl.BlockSpec((1, n), lambda i: (0, 0))],
        out_specs=[pl.BlockSpec((tr, n), lambda i: (i, 0))] * 2,
        out_shape=[jax.ShapeDtypeStruct((r, n), F32)] * 2,
        compiler_params=_cparams("parallel"),
        name="rope_table",
    )(pos, freq)


FF_CHUNK = 256


def _ffn_kernel(x_ref, wg_ref, wu_ref, wo_ref, g_ref, b_ref, o_ref, acc_ref):
    x = x_ref[...]
    xb = x.astype(BF16)
    for c in range(D_FF // FF_CHUNK):
        sl = slice(c * FF_CHUNK, (c + 1) * FF_CHUNK)
        gate = _dot(xb, wg_ref[:, sl])
        up = _dot(xb, wu_ref[:, sl])
        hid = (gate * jax.nn.sigmoid(gate) * up).astype(BF16)
        part = _dot(hid, wo_ref[sl, :])
        if c == 0:
            acc_ref[...] = part
        else:
            acc_ref[...] += part
    y = DEEPNORM_ALPHA * x + 0.5 * acc_ref[...]
    o_ref[...] = _layernorm(y, g_ref[...], b_ref[...])


def _ffn(x, wg, wu, wo, g, b, tm):
    n = x.shape[0]
    row = pl.BlockSpec((tm, D_MODEL), lambda i: (i, 0))
    return pl.pallas_call(
        _ffn_kernel,
        grid=(n // tm,),
        in_specs=[row, _resident(wg.shape), _resident(wu.shape), _resident(wo.shape),
                  _resident(g.shape), _resident(b.shape)],
        out_specs=row,
        out_shape=jax.ShapeDtypeStruct((n, D_MODEL), F32),
        scratch_shapes=[pltpu.VMEM((tm, D_MODEL), F32)],
        compiler_params=_cparams("parallel"),
        name="ffn",
    )(x, wg, wu, wo, g, b)


def _proj_kernel(x_ref, cos_ref, sin_ref, w1_ref, w1r_ref, qn_ref, wuq_ref, wuqr_ref, kvn_ref, wuk_ref,
                 qa_ref, qi_ref, wi_ref, qcat_ref, dsa_ref, mla_ref):
    xb = x_ref[...].astype(BF16)
    cos = cos_ref[...]
    sin = sin_ref[...]
    z = _dot(xb, w1_ref[...]) * cos[:, :N1P] + _dot(xb, w1r_ref[...]) * sin[:, :N1P]
    for h in range(A_HEADS):
        qa_ref[h] = z[:, C_AQ + h * A_HEAD_DIM:C_AQ + (h + 1) * A_HEAD_DIM]
    for h in range(IDX_HEADS):
        qi_ref[h] = z[:, C_IQ + h * IDX_DIM:C_IQ + (h + 1) * IDX_DIM]
    wi_ref[...] = z[:, C_IW:C_IW + IDX_HEADS] * ((IDX_HEADS * IDX_DIM) ** -0.5)
    dsa_ref[...] = z[:, C_AKV:C_AKV + ROW]
    mla_ref[:, :KV_LORA] = _rmsnorm(z[:, C_CKV:C_CKV + KV_LORA], kvn_ref[...])
    mla_ref[:, KV_LORA:] = z[:, C_KPE:C_KPE + QK_ROPE]
    cq = _rmsnorm(z[:, C_CQ:C_CQ + Q_LORA], qn_ref[...]).astype(BF16)
    qb = _dot(cq, wuq_ref[...])
    n_nope = B_HEADS * QK_NOPE
    q_pe = qb[:, n_nope:] * cos[:, C_PE:] + _dot(cq, wuqr_ref[...]) * sin[:, C_PE:]
    for h in range(B_HEADS):
        q_nope = qb[:, h * QK_NOPE:(h + 1) * QK_NOPE].astype(BF16)
        qcat_ref[h, :, :KV_LORA] = _dot(q_nope, wuk_ref[h])
        qcat_ref[h, :, KV_LORA:] = q_pe[:, h * QK_ROPE:(h + 1) * QK_ROPE]


def _project(x, cos, sin, tab_blocks, w, tm):
    n = x.shape[0]
    w1, w1r, qn, wuq, wuqr, kvn, wukt = w
    row = lambda width: pl.BlockSpec((tm, width), lambda i: (i, 0))
    heads = lambda width: pl.BlockSpec((8, tm, width), lambda i: (0, i, 0))
    tab = pl.BlockSpec((tm, N_TAB), lambda i: (i % tab_blocks, 0))
    return pl.pallas_call(
        _proj_kernel,
        grid=(n // tm,),
        in_specs=[row(D_MODEL), tab, tab, _resident(w1.shape), _resident(w1r.shape), _resident(qn.shape),
                  _resident(wuq.shape), _resident(wuqr.shape), _resident(kvn.shape), _resident(wukt.shape)],
        out_specs=[heads(A_HEAD_DIM), heads(IDX_DIM), row(IDX_HEADS), heads(ROW), row(ROW), row(ROW)],
        out_shape=[jax.ShapeDtypeStruct((A_HEADS, n, A_HEAD_DIM), F32),
                   jax.ShapeDtypeStruct((IDX_HEADS, n, IDX_DIM), F32),
                   jax.ShapeDtypeStruct((n, IDX_HEADS), F32),
                   jax.ShapeDtypeStruct((B_HEADS, n, ROW), F32),
                   jax.ShapeDtypeStruct((n, ROW), F32),
                   jax.ShapeDtypeStruct((n, ROW), F32)],
        compiler_params=_cparams("parallel"),
        name="project",
    )(x, cos, sin, w1, w1r, qn, wuq, wuqr, kvn, wukt)


def _merge_kernel(x_ref, oa_ref, olat_ref, wuv_ref, woa_ref, wob_ref, g_ref, b_ref, o_ref):
    ob = [_dot(olat_ref[:, h * KV_LORA:(h + 1) * KV_LORA].astype(BF16), wuv_ref[h]) for h in range(B_HEADS)]
    ob = jnp.concatenate(ob, axis=-1).astype(BF16)
    mix = _dot(oa_ref[...].astype(BF16), woa_ref[...]) + _dot(ob, wob_ref[...])
    o_ref[...] = _layernorm(DEEPNORM_ALPHA * x_ref[...] + mix, g_ref[...], b_ref[...])


def _merge(x, oa, olat, wuv, woa, wob, g, b, tm):
    n = x.shape[0]
    row = lambda width: pl.BlockSpec((tm, width), lambda i: (i, 0))
    return pl.pallas_call(
        _merge_kernel,
        grid=(n // tm,),
        in_specs=[row(D_MODEL), row(A_HEADS * A_HEAD_DIM), row(B_HEADS * KV_LORA), _resident(wuv.shape),
                  _resident(woa.shape), _resident(wob.shape), _resident(g.shape), _resident(b.shape)],
        out_specs=row(D_MODEL),
        out_shape=jax.ShapeDtypeStruct((n, D_MODEL), F32),
        compiler_params=_cparams("parallel"),
        name="merge",
    )(x, oa, olat, wuv, woa, wob, g, b)


LANE = 128
SELECT_BITS = 32
SELECT_UNROLL = 4


def _order_key(score):
    bits = lax.bitcast_convert_type(score + 0.0, jnp.int32)
    return bits ^ ((bits >> 31) & jnp.int32(0x7FFFFFFF))


def _count_ge(keys, cand):
    return jnp.sum(jnp.where(keys >= cand, 1.0, 0.0), axis=1, keepdims=True)


def _kth_largest_key(key_ref, top_k, unroll):
    rows = key_ref.shape[0]
    if unroll:
        t = jnp.full((rows, 1), INT_MIN, jnp.int32)
        for hi in range(SELECT_BITS - 1, 0, -2):
            keys = key_ref[...]
            b_hi = jnp.int32(INT_MIN if hi == 31 else 1 << hi)
            b_lo = jnp.int32(1 << (hi - 1))
            c1, c2 = t ^ b_lo, t ^ b_hi
            c3 = c2 ^ b_lo
            n1, n2, n3 = _count_ge(keys, c1), _count_ge(keys, c2), _count_ge(keys, c3)
            t = jnp.where(n3 >= top_k, c3, jnp.where(n2 >= top_k, c2, jnp.where(n1 >= top_k, c1, t)))
        return t

    def body(it, t):
        bit = lax.shift_left(jnp.int32(1), jnp.int32(SELECT_BITS - 1) - it)
        cand = t ^ bit
        return jnp.where(_count_ge(key_ref[...], cand) >= top_k, cand, t)

    return lax.fori_loop(0, SELECT_BITS, body, jnp.full((rows, 1), INT_MIN, jnp.int32), unroll=SELECT_UNROLL)


def _select_bias(key_ref, bias_ref, top_k, lane_order, unroll=False):
    kth = jnp.maximum(_kth_largest_key(key_ref, top_k, unroll), jnp.int32(INT_MIN + 1))
    keys = key_ref[...]
    bias_ref[...] = jnp.where(keys >= kth, 0.0, NEG_INF)
    n_ge = _count_ge(keys, kth)

    @pl.when(jnp.max(n_ge) > top_k)
    def _():
        ks = key_ref[...]
        need = top_k - jnp.sum(jnp.where(ks > kth, 1.0, 0.0), axis=1, keepdims=True)
        r_i = lax.broadcasted_iota(jnp.int32, (LANE, LANE), 0)
        c_i = lax.broadcasted_iota(jnp.int32, (LANE, LANE), 1)
        tri = jnp.where(r_i <= c_i, 1.0, 0.0).astype(BF16)
        seen = jnp.zeros_like(need)
        for blk in lane_order:
            sl = slice(blk * LANE, (blk + 1) * LANE)
            kb = key_ref[:, sl]
            tie = kb == kth
            tie_f = jnp.where(tie, 1.0, 0.0)
            rank = _dot(tie_f.astype(BF16), tri) + seen
            keep = (kb > kth) | (tie & (rank <= need))
            bias_ref[:, sl] = jnp.where(keep, 0.0, NEG_INF)
            seen = seen + jnp.sum(tie_f, axis=1, keepdims=True)


LOG2_E = 1.4426950408889634
MLA_SCALE_LOG2 = (QK_NOPE + QK_ROPE) ** -0.5 * LOG2_E


def _softmax_rows(s, base2=False):
    m = jnp.max(s, axis=-1, keepdims=True)
    p = jnp.exp2(s - m) if base2 else jnp.exp(s - m)
    return p, 1.0 / jnp.sum(p, axis=-1, keepdims=True)


def _prompt_positions(tq, n_real, q_pos0):
    lk = n_real + LANE
    lane = lax.broadcasted_iota(jnp.int32, (1, lk), 1)
    big = jnp.int32(2 ** 30)
    key_pos = jnp.where(lane < n_real, lane + N_META, jnp.where(lane < n_real + N_META, lane - n_real, big))
    q_pos = lax.broadcasted_iota(jnp.int32, (tq, 1), 0) + q_pos0
    return key_pos <= q_pos


def _gather_keys(real_ref, meta_ref, n_real):
    if n_real == 0:
        return meta_ref[...].astype(BF16)
    return jnp.concatenate([real_ref[...], meta_ref[...]], axis=0).astype(BF16)


def _dsa_prompt_kernel(qa_ref, qi_ref, wi_ref, real_ref, meta_ref, o_ref, key_ref, bias_ref, *,
                       tq, n_real, q_pos0, top_k):
    kb = _gather_keys(real_ref, meta_ref, n_real)
    k_idx = kb[:, A_KV:]
    wi = wi_ref[...]
    score = None
    for h in range(IDX_HEADS):
        d = jnp.maximum(_dot_nt(qi_ref[h].astype(BF16), k_idx), 0.0) * wi[:, h:h + 1]
        score = d if score is None else score + d
    adm = _prompt_positions(tq, n_real, q_pos0)
    key_ref[...] = jnp.where(adm, _order_key(score), jnp.int32(INT_MIN))
    n_blk = n_real // LANE
    _select_bias(key_ref, bias_ref, top_k, [n_blk] + list(range(n_blk)))
    k_att = kb[:, :A_HEAD_DIM]
    v_att = kb[:, A_HEAD_DIM:A_KV]
    for h in range(A_HEADS):
        s = _dot_nt((qa_ref[h] * (A_HEAD_DIM ** -0.5)).astype(BF16), k_att) + bias_ref[...]
        p, inv = _softmax_rows(s)
        o_ref[:, h * A_HEAD_DIM:(h + 1) * A_HEAD_DIM] = _dot(p.astype(BF16), v_att) * inv


def _mla_prompt_kernel(q_ref, real_ref, meta_ref, o_ref, *, tq, n_real, q_pos0):
    kb = _gather_keys(real_ref, meta_ref, n_real)
    c_kv = kb[:, :KV_LORA]
    bias = jnp.where(_prompt_positions(tq, n_real, q_pos0), 0.0, NEG_INF)
    for h in range(B_HEADS):
        s = _dot_nt(q_ref[h].astype(BF16), kb) * MLA_SCALE_LOG2 + bias
        p, inv = _softmax_rows(s, base2=True)
        o_ref[:, h * KV_LORA:(h + 1) * KV_LORA] = _dot(p.astype(BF16), c_kv) * inv


def _prompt_attention(qa, qi, wi, qcat, dsa_real, mla_real, dsa_meta, mla_meta, *, blk, tq, q_pos0, top_k):
    nb = qa.shape[1]
    has_real = dsa_real is not None
    n_real = (blk + 1) * tq if has_real else 0
    lk = n_real + LANE
    if not has_real:
        dsa_real = jnp.zeros((nb, 8, ROW), F32)
        mla_real = dsa_real
    nr_blk = max(n_real, 8)
    heads = lambda width: pl.BlockSpec((8, None, tq, width), lambda b: (0, b, blk, 0))
    rows = lambda width: pl.BlockSpec((None, tq, width), lambda b: (b, blk, 0))
    real = pl.BlockSpec((None, nr_blk, ROW), lambda b: (b, 0, 0))
    meta = _resident((LANE, ROW))
    pos0 = q_pos0 + blk * tq
    o_a = pl.pallas_call(
        functools.partial(_dsa_prompt_kernel, tq=tq, n_real=n_real, q_pos0=pos0, top_k=top_k),
        grid=(nb,),
        in_specs=[heads(A_HEAD_DIM), heads(IDX_DIM), rows(IDX_HEADS), real, meta],
        out_specs=pl.BlockSpec((None, tq, A_HEADS * A_HEAD_DIM), lambda b: (b, 0, 0)),
        out_shape=jax.ShapeDtypeStruct((nb, tq, A_HEADS * A_HEAD_DIM), F32),
        scratch_shapes=[pltpu.VMEM((tq, lk), jnp.int32), pltpu.VMEM((tq, lk), F32)],
        compiler_params=_cparams("parallel"),
        name=f"dsa_prompt_{n_real}",
    )(qa, qi, wi, dsa_real, dsa_meta)
    o_lat = pl.pallas_call(
        functools.partial(_mla_prompt_kernel, tq=tq, n_real=n_real, q_pos0=pos0),
        grid=(nb,),
        in_specs=[heads(ROW), real, meta],
        out_specs=pl.BlockSpec((None, tq, B_HEADS * KV_LORA), lambda b: (b, 0, 0)),
        out_shape=jax.ShapeDtypeStruct((nb, tq, B_HEADS * KV_LORA), F32),
        compiler_params=_cparams("parallel"),
        name=f"mla_prompt_{n_real}",
    )(qcat, mla_real, mla_meta)
    return o_a, o_lat


def _sample_kernel(pt_ref, qa_ref, qi_ref, wi_ref, qcat_ref, dsa_new_ref, mla_new_ref, dsa_hbm, mla_hbm,
                   oa_ref, olat_ref, dsa_buf, mla_buf, sems, key_ref, bias_ref, *, layer, n_pages, t_new, top_k):
    b = pl.program_id(0)
    nb = pl.num_programs(0)
    slot = b % 2
    past = n_pages * PAGE_SIZE
    lk = past + PAGE_SIZE

    def page_copies(seq, slt):
        out = []
        for p in range(n_pages):
            page = pt_ref[seq * n_pages + p]
            window = pl.ds(p * PAGE_SIZE, PAGE_SIZE)
            out.append(pltpu.make_async_copy(dsa_hbm.at[layer, page], dsa_buf.at[slt, :, window], sems.at[0, slt]))
            out.append(pltpu.make_async_copy(mla_hbm.at[layer, page], mla_buf.at[slt, :, window], sems.at[1, slt]))
        return out

    @pl.when(b == 0)
    def _():
        for c in page_copies(b, slot):
            c.start()

    @pl.when(b + 1 < nb)
    def _():
        for c in page_copies(b + 1, 1 - slot):
            c.start()

    dsa_buf[slot, :, past:] = dsa_new_ref[...]
    mla_buf[slot, :, past:] = mla_new_ref[...]

    for c in page_copies(b, slot):
        c.wait()

    n_row = A_HEADS * t_new
    lane = lax.broadcasted_iota(jnp.int32, (1, lk), 1)
    q_id = lax.broadcasted_iota(jnp.int32, (t_new, 1), 0)
    adm = lane <= q_id + past

    def per_head(bias):
        return jnp.concatenate([bias] * A_HEADS, axis=0)

    idx_t = dsa_buf[slot, A_KV:, :].astype(BF16)
    d = jnp.maximum(_dot(qi_ref[...].reshape(n_row, IDX_DIM).astype(BF16), idx_t), 0.0) * wi_ref[...]
    score = d[0:t_new]
    for h in range(1, IDX_HEADS):
        score = score + d[h * t_new:(h + 1) * t_new]
    key_ref[...] = jnp.where(adm, _order_key(score), jnp.int32(INT_MIN))

    km = mla_buf[slot].astype(BF16)
    s = _dot(qcat_ref[...].reshape(n_row, ROW).astype(BF16), km) * MLA_SCALE_LOG2
    p, inv = _softmax_rows(s + per_head(jnp.where(adm, 0.0, NEG_INF)), base2=True)
    o = _dot_nt(p.astype(BF16), km[:KV_LORA]) * inv
    for h in range(B_HEADS):
        olat_ref[:, h * KV_LORA:(h + 1) * KV_LORA] = o[h * t_new:(h + 1) * t_new]

    _select_bias(key_ref, bias_ref, top_k, list(range(lk // LANE)), unroll=True)

    k_t = dsa_buf[slot, :A_HEAD_DIM, :].astype(BF16)
    v_t = dsa_buf[slot, A_HEAD_DIM:A_KV, :].astype(BF16)
    qa = (qa_ref[...].reshape(n_row, A_HEAD_DIM) * (A_HEAD_DIM ** -0.5)).astype(BF16)
    p, inv = _softmax_rows(_dot(qa, k_t) + per_head(bias_ref[...]))
    o = _dot_nt(p.astype(BF16), v_t) * inv
    for h in range(A_HEADS):
        oa_ref[:, h * A_HEAD_DIM:(h + 1) * A_HEAD_DIM] = o[h * t_new:(h + 1) * t_new]


def _sample_attention(page_table, qa, qi, wi_col, qcat, dsa_new_t, mla_new_t, cache_dsa_t, cache_mla_t, *, layer,
                      top_k):
    nb, n_pages = page_table.shape
    t_new = qa.shape[2]
    lk = (n_pages + 1) * PAGE_SIZE
    heads = lambda width: pl.BlockSpec((8, None, t_new, width), lambda b, pt: (0, b, 0, 0))
    rows = lambda n, width: pl.BlockSpec((None, n, width), lambda b, pt: (b, 0, 0))
    grid_spec = pltpu.PrefetchScalarGridSpec(
        num_scalar_prefetch=1,
        grid=(nb,),
        in_specs=[heads(A_HEAD_DIM), heads(IDX_DIM), rows(8 * t_new, 1), heads(ROW), rows(ROW, PAGE_SIZE),
                  rows(ROW, PAGE_SIZE), pl.BlockSpec(memory_space=pl.ANY), pl.BlockSpec(memory_space=pl.ANY)],
        out_specs=[rows(t_new, A_HEADS * A_HEAD_DIM), rows(t_new, B_HEADS * KV_LORA)],
        scratch_shapes=[pltpu.VMEM((2, ROW, lk), F32),
                        pltpu.VMEM((2, ROW, lk), F32),
                        pltpu.SemaphoreType.DMA((2, 2)),
                        pltpu.VMEM((t_new, lk), jnp.int32),
                        pltpu.VMEM((t_new, lk), F32)],
    )
    return pl.pallas_call(
        functools.partial(_sample_kernel, layer=layer, n_pages=n_pages, t_new=t_new, top_k=top_k),
        grid_spec=grid_spec,
        out_shape=[jax.ShapeDtypeStruct((nb, t_new, A_HEADS * A_HEAD_DIM), F32),
                   jax.ShapeDtypeStruct((nb, t_new, B_HEADS * KV_LORA), F32)],
        compiler_params=_cparams("arbitrary"),
        name="sample_attention",
    )(page_table.reshape(-1), qa, qi, wi_col, qcat, dsa_new_t, mla_new_t, cache_dsa_t, cache_mla_t)


def _swap_signed(w, heads, head_dim, rot):
    k = w.shape[0]
    w3 = w.reshape(k, heads, head_dim)
    half = rot // 2
    rest = jnp.zeros((k, heads, head_dim - rot), w.dtype)
    return jnp.concatenate([-w3[:, :, half:rot], w3[:, :, :half], rest], axis=2).reshape(k, heads * head_dim)


def _prep_mixer_weights(w_in_l, q_norm_l, w_uq_l, kv_norm_l, w_uk_l, w_uv_l, w_o_l):
    col = lambda start, n: w_in_l[:, start:start + n]
    zero = lambda n: jnp.zeros((D_MODEL, n), w_in_l.dtype)
    aq, akv, iq, ik = col(O_AQ, 512), col(O_AKV, A_KV), col(O_IQ, 256), col(O_IK, IDX_DIM)
    iw, cq, ckv, kpe = col(O_IW, IDX_HEADS), col(O_CQ, Q_LORA), col(O_CKV, KV_LORA), col(O_KPE, QK_ROPE)
    gap, tail = zero(C_CKV - C_IK - IDX_DIM), zero(N1P - C_IW - IDX_HEADS)
    w1 = jnp.concatenate([aq, iq, cq, akv, ik, gap, ckv, kpe, iw, tail], axis=1)
    w1r = jnp.concatenate([_swap_signed(aq, A_HEADS, A_HEAD_DIM, A_ROT), _swap_signed(iq, IDX_HEADS, IDX_DIM, IDX_ROT),
                           zero(Q_LORA), _swap_signed(akv[:, :A_HEAD_DIM], 1, A_HEAD_DIM, A_ROT), zero(A_HEAD_DIM),
                           _swap_signed(ik, 1, IDX_DIM, IDX_ROT), gap, zero(KV_LORA),
                           _swap_signed(kpe, 1, QK_ROPE, QK_ROPE), zero(IDX_HEADS), tail], axis=1)
    uq = w_uq_l.reshape(Q_LORA, B_HEADS, QK_NOPE + QK_ROPE)
    uq_nope = uq[:, :, :QK_NOPE].reshape(Q_LORA, B_HEADS * QK_NOPE)
    uq_pe = uq[:, :, QK_NOPE:].reshape(Q_LORA, B_HEADS * QK_ROPE)
    uq_pe_r = _swap_signed(uq_pe, B_HEADS, QK_ROPE, QK_ROPE)
    wuq = jnp.concatenate([uq_nope, uq_pe], axis=1)
    wukt = jnp.transpose(w_uk_l, (1, 2, 0))
    wuv = jnp.transpose(w_uv_l, (1, 0, 2))
    n_a = A_HEADS * A_HEAD_DIM
    proj = (w1.astype(BF16), w1r.astype(BF16), q_norm_l.reshape(1, -1), wuq.astype(BF16), uq_pe_r.astype(BF16),
            kv_norm_l.reshape(1, -1), wukt.astype(BF16))
    merge = (wuv.astype(BF16), w_o_l[:n_a].astype(BF16), w_o_l[n_a:].astype(BF16))
    return proj, merge


def _prep_ffn_weights(w_in_l, w_out_l):
    return w_in_l[:, :D_FF].astype(BF16), w_in_l[:, D_FF:].astype(BF16), w_out_l.astype(BF16)


def _token_tile(n, cap):
    best = 8
    for t in range(8, cap + 1, 8):
        if n % t == 0:
            best = t
    return best


PROMPT_TQ = 256
PROMPT_TM = 512


def kernel(x_prompt, x_sample, cache_dsa, cache_mla, page_table, meta_tokens, ln1_g, ln1_b, ffn1_w_in, ffn1_w_out,
           w_in, mla_q_norm, mla_w_uq, mla_kv_norm, mla_w_uk, mla_w_uv, w_o, ln2_g, ln2_b, ffn2_w_in, ffn2_w_out,
           ln3_g, ln3_b):
    nb, seq, _ = x_prompt.shape
    db, t_new, _ = x_sample.shape
    n_pages = page_table.shape[1]
    past = n_pages * PAGE_SIZE
    n_p = nb * seq
    n_s = db * t_new
    n_g = n_s + N_META
    tq = min(PROMPT_TQ, seq)
    tm_p = _token_tile(n_p, PROMPT_TM)
    tm_p = tm_p if seq % tm_p == 0 else _token_tile(seq, PROMPT_TM)
    tm_g = _token_tile(n_g, 1024)
    topk_p = min(TOPK_MAX, (seq + N_META) // 4)
    topk_s = min(TOPK_MAX, (past + t_new) // 4)

    freq = _rope_freqs()
    pos_p = (jnp.arange(seq, dtype=jnp.int32) + N_META).astype(F32).reshape(seq, 1)
    pos_g = jnp.concatenate([jnp.tile(past + jnp.arange(t_new, dtype=jnp.int32), db),
                             jnp.arange(N_META, dtype=jnp.int32)]).astype(F32).reshape(n_g, 1)
    cos_p, sin_p = _rope_table(pos_p, freq, _token_tile(seq, 256))
    cos_g, sin_g = _rope_table(pos_g, freq, _token_tile(n_g, 256))

    cache_dsa_t = jnp.swapaxes(cache_dsa, 2, 3)
    cache_mla_t = jnp.swapaxes(cache_mla, 2, 3)

    xp = x_prompt.reshape(n_p, D_MODEL)
    xg = jnp.concatenate([x_sample.reshape(n_s, D_MODEL), meta_tokens.astype(x_prompt.dtype)], axis=0)
    vec = lambda v: v.reshape(1, -1)
    rows_out = [[], [], [], []]
    for l in range(DEPTH):
        f1 = _prep_ffn_weights(ffn1_w_in[l], ffn1_w_out[l])
        f2 = _prep_ffn_weights(ffn2_w_in[l], ffn2_w_out[l])
        pw, mw = _prep_mixer_weights(w_in[l], mla_q_norm[l], mla_w_uq[l], mla_kv_norm[l], mla_w_uk[l],
                                     mla_w_uv[l], w_o[l])
        xp = _ffn(xp, *f1, vec(ln1_g[l]), vec(ln1_b[l]), tm_p)
        xg = _ffn(xg, *f1, vec(ln1_g[l]), vec(ln1_b[l]), tm_g)

        qa_p, qi_p, wi_p, qcat_p, dsa_p, mla_p = _project(xp, cos_p, sin_p, seq // tm_p, pw, tm_p)
        qa_g, qi_g, wi_g, qcat_g, dsa_g, mla_g = _project(xg, cos_g, sin_g, n_g // tm_g, pw, tm_g)

        zpad = jnp.zeros((LANE - N_META, ROW), F32)
        dsa_meta = jnp.concatenate([dsa_g[n_s:], zpad], axis=0)
        mla_meta = jnp.concatenate([mla_g[n_s:], zpad], axis=0)
        hm = lambda a: a[:, n_s:].reshape(8, 1, N_META, a.shape[-1])
        oa_m, olat_m = _prompt_attention(hm(qa_g), hm(qi_g), wi_g[n_s:].reshape(1, N_META, IDX_HEADS), hm(qcat_g),
                                         None, None, dsa_meta, mla_meta, blk=0, tq=N_META, q_pos0=0, top_k=topk_p)

        hp = lambda a: a.reshape(8, nb, seq, a.shape[-1])
        dsa_p3 = dsa_p.reshape(nb, seq, ROW)
        mla_p3 = mla_p.reshape(nb, seq, ROW)
        oa_blocks, olat_blocks = [], []
        for blk in range(seq // tq):
            oa_b, olat_b = _prompt_attention(hp(qa_p), hp(qi_p), wi_p.reshape(nb, seq, IDX_HEADS), hp(qcat_p),
                                             dsa_p3, mla_p3, dsa_meta, mla_meta, blk=blk, tq=tq, q_pos0=N_META,
                                             top_k=topk_p)
            oa_blocks.append(oa_b)
            olat_blocks.append(olat_b)
        oa_p = jnp.concatenate(oa_blocks, axis=1).reshape(n_p, -1)
        olat_p = jnp.concatenate(olat_blocks, axis=1).reshape(n_p, -1)

        hs = lambda a: a[:, :n_s].reshape(8, db, t_new, a.shape[-1])
        wi_col = jnp.transpose(wi_g[:n_s].reshape(db, t_new, IDX_HEADS), (0, 2, 1)).reshape(db, IDX_HEADS * t_new, 1)
        dsa_s = dsa_g[:n_s].reshape(db, t_new, ROW)
        mla_s = mla_g[:n_s].reshape(db, t_new, ROW)
        new_t = lambda r: jnp.pad(jnp.swapaxes(r, 1, 2), ((0, 0), (0, 0), (0, PAGE_SIZE - t_new)))
        oa_s, olat_s = _sample_attention(page_table, hs(qa_g), hs(qi_g), wi_col, hs(qcat_g), new_t(dsa_s),
                                         new_t(mla_s), cache_dsa_t, cache_mla_t, layer=l, top_k=topk_s)
        oa_g = jnp.concatenate([oa_s.reshape(n_s, -1), oa_m.reshape(N_META, -1)], axis=0)
        olat_g = jnp.concatenate([olat_s.reshape(n_s, -1), olat_m.reshape(N_META, -1)], axis=0)

        xp = _merge(xp, oa_p, olat_p, *mw, vec(ln2_g[l]), vec(ln2_b[l]), tm_p)
        xg = _merge(xg, oa_g, olat_g, *mw, vec(ln2_g[l]), vec(ln2_b[l]), tm_g)
        xp = _ffn(xp, *f2, vec(ln3_g[l]), vec(ln3_b[l]), tm_p)
        xg = _ffn(xg, *f2, vec(ln3_g[l]), vec(ln3_b[l]), tm_g)

        bc = lambda m: jnp.broadcast_to(m[None, :N_META], (nb, N_META, ROW))
        rows_out[0].append(jnp.concatenate([bc(dsa_meta), dsa_p3], axis=1))
        rows_out[1].append(jnp.concatenate([bc(mla_meta), mla_p3], axis=1))
        rows_out[2].append(dsa_s)
        rows_out[3].append(mla_s)

    y_prompt = xp.reshape(nb, seq, D_MODEL)
    y_sample = xg[:n_s].reshape(db, t_new, D_MODEL)
    return (y_prompt, y_sample, jnp.stack(rows_out[0]), jnp.stack(rows_out[1]), jnp.stack(rows_out[2]),
            jnp.stack(rows_out[3]))
```

```python
import functools

import numpy as np
import jax
import jax.numpy as jnp
from jax import lax
from jax.experimental import pallas as pl
from jax.experimental.pallas import tpu as pltpu

D_MODEL = 1024
DEPTH = 2
PAGE_SIZE = 128
N_META = 16
A_HEADS = 8
A_HEAD_DIM = 64
A_ROT = 16
A_KV = 128
IDX_HEADS = 8
IDX_DIM = 32
IDX_ROT = 8
TOPK_MAX = 256
B_HEADS = 8
Q_LORA = 256
KV_LORA = 128
QK_NOPE = 64
QK_ROPE = 32
V_DIM = 64
D_FF = 2816
ROPE_THETA = 500000.0
DEEPNORM_ALPHA = (2 * DEPTH) ** 0.25
LN_EPS = 1e-5
RMS_EPS = 1e-6
ROW = 160

BF16 = jnp.bfloat16
F32 = jnp.float32
INT_MIN = -(2 ** 31)
NEG_INF = float("-inf")

VMEM_LIMIT = 56 * 1024 * 1024

C_AQ = 0
C_IQ = 512
C_CQ = 768
C_AKV = 1024
C_IK = 1152
C_CKV = 1280
C_KPE = 1408
C_IW = 1440
N1P = 1536
C_PE = N1P
N_TAB = N1P + B_HEADS * QK_ROPE

O_AQ, O_AKV, O_IQ, O_IK, O_IW, O_CQ, O_CKV, O_KPE = 0, 512, 640, 896, 928, 936, 1192, 1320


def _layout_tables():
    fidx = np.full((N_TAB,), -1, np.int64)
    half_of = np.ones((N_TAB,), np.int64)
    for start, heads, hd, rot in ((C_AQ, A_HEADS, A_HEAD_DIM, A_ROT), (C_AKV, 1, A_HEAD_DIM, A_ROT),
                                  (C_IQ, IDX_HEADS, IDX_DIM, IDX_ROT), (C_IK, 1, IDX_DIM, IDX_ROT),
                                  (C_KPE, 1, QK_ROPE, QK_ROPE), (C_PE, B_HEADS, QK_ROPE, QK_ROPE)):
        half = rot // 2
        for h in range(heads):
            for d in range(rot):
                c = start + h * hd + d
                fidx[c] = d % half
                half_of[c] = half
    return fidx, half_of


_FIDX, _HALF = _layout_tables()


def _rope_freqs():
    f = jnp.zeros((N_TAB,), F32)
    for half in (A_ROT // 2, IDX_ROT // 2, QK_ROPE // 2):
        inv = ROPE_THETA ** (-jnp.arange(half, dtype=F32) / half)
        sel = (_HALF == half) & (_FIDX >= 0)
        cols = np.nonzero(sel)[0]
        f = f.at[cols].set(inv[_FIDX[cols]])
    return f.reshape(1, N_TAB)


def _cparams(*sem):
    return pltpu.CompilerParams(dimension_semantics=sem, vmem_limit_bytes=VMEM_LIMIT)


def _resident(shape):
    nd = len(shape)
    return pl.BlockSpec(shape, lambda *_: (0,) * nd, pipeline_mode=pl.Buffered(1))


def _layernorm(y, g, b):
    mu = jnp.mean(y, axis=-1, keepdims=True)
    yc = y - mu
    var = jnp.mean(yc * yc, axis=-1, keepdims=True)
    return yc * lax.rsqrt(var + LN_EPS) * g + b


def _rmsnorm(y, g):
    return y * lax.rsqrt(jnp.mean(y * y, axis=-1, keepdims=True) + RMS_EPS) * g


def _dot(a, b):
    return jnp.dot(a, b, preferred_element_type=F32)


def _dot_nt(a, b):
    return lax.dot_general(a, b, (((1,), (1,)), ((), ())), preferred_element_type=F32)


def _rope_table_kernel(pos_ref, freq_ref, cos_ref, sin_ref):
    ang = pos_ref[...] * freq_ref[...]
    cos_ref[...] = jnp.cos(ang)
    sin_ref[...] = jnp.sin(ang)


def _rope_table(pos, freq, tr):
    r, n = pos.shape[0], freq.shape[1]
    return pl.pallas_call(
        _rope_table_kernel,
        grid=(r // tr,),
        in_specs=[pl.BlockSpec((tr, 1), lambda i: (i, 0)), pl.BlockSpec((1, n), lambda i: (0, 0))],
        out_specs=[pl.BlockSpec((tr, n), lambda i: (i, 0))] * 2,
        out_shape=[jax.ShapeDtypeStruct((r, n), F32)] * 2,
        compiler_params=_cparams("parallel"),
        name="rope_table",
    )(pos, freq)


FF_CHUNK = 256


def _ffn_kernel(x_ref, wg_ref, wu_ref, wo_ref, g_ref, b_ref, o_ref, acc_ref):
    x = x_ref[...]
    xb = x.astype(BF16)
    for c in range(D_FF // FF_CHUNK):
        sl = slice(c * FF_CHUNK, (c + 1) * FF_CHUNK)
        gate = _dot(xb, wg_ref[:, sl])
        up = _dot(xb, wu_ref[:, sl])
        hid = (gate * jax.nn.sigmoid(gate) * up).astype(BF16)
        part = _dot(hid, wo_ref[sl, :])
        if c == 0:
            acc_ref[...] = part
        else:
            acc_ref[...] += part
    y = DEEPNORM_ALPHA * x + 0.5 * acc_ref[...]
    o_ref[...] = _layernorm(y, g_ref[...], b_ref[...])


def _ffn(x, wg, wu, wo, g, b, tm):
    n = x.shape[0]
    row = pl.BlockSpec((tm, D_MODEL), lambda i: (i, 0))
    return pl.pallas_call(
        _ffn_kernel,
        grid=(n // tm,),
        in_specs=[row, _resident(wg.shape), _resident(wu.shape), _resident(wo.shape),
                  _resident(g.shape), _resident(b.shape)],
        out_specs=row,
        out_shape=jax.ShapeDtypeStruct((n, D_MODEL), F32),
        scratch_shapes=[pltpu.VMEM((tm, D_MODEL), F32)],
        compiler_params=_cparams("parallel"),
        name="ffn",
    )(x, wg, wu, wo, g, b)


PROJ_SPLIT = 2


def _proj_kernel(x_ref, cos_ref, sin_ref, w1_ref, w1r_ref, qn_ref, wuq_ref, wuqr_ref, kvn_ref, wuk_ref,
                 qa_ref, qi_ref, wi_ref, qcat_ref, dsa_ref, mla_ref):
    half = x_ref.shape[0] // PROJ_SPLIT
    for part in range(PROJ_SPLIT):
        rows = slice(part * half, (part + 1) * half)
        xb = x_ref[rows, :].astype(BF16)
        cos = cos_ref[rows, :]
        sin = sin_ref[rows, :]
        z = _dot(xb, w1_ref[...]) * cos[:, :N1P] + _dot(xb, w1r_ref[...]) * sin[:, :N1P]
        for h in range(A_HEADS):
            qa_ref[h, rows, :] = z[:, C_AQ + h * A_HEAD_DIM:C_AQ + (h + 1) * A_HEAD_DIM]
        for h in range(IDX_HEADS):
            qi_ref[h, rows, :] = z[:, C_IQ + h * IDX_DIM:C_IQ + (h + 1) * IDX_DIM]
        wi_ref[rows, :] = z[:, C_IW:C_IW + IDX_HEADS] * ((IDX_HEADS * IDX_DIM) ** -0.5)
        dsa_ref[rows, :] = z[:, C_AKV:C_AKV + ROW]
        mla_ref[rows, :KV_LORA] = _rmsnorm(z[:, C_CKV:C_CKV + KV_LORA], kvn_ref[...])
        mla_ref[rows, KV_LORA:] = z[:, C_KPE:C_KPE + QK_ROPE]
        cq = _rmsnorm(z[:, C_CQ:C_CQ + Q_LORA], qn_ref[...]).astype(BF16)
        qb = _dot(cq, wuq_ref[...])
        n_nope = B_HEADS * QK_NOPE
        q_pe = qb[:, n_nope:] * cos[:, C_PE:] + _dot(cq, wuqr_ref[...]) * sin[:, C_PE:]
        for h in range(B_HEADS):
            q_nope = qb[:, h * QK_NOPE:(h + 1) * QK_NOPE].astype(BF16)
            qcat_ref[h, rows, :KV_LORA] = _dot(q_nope, wuk_ref[h])
            qcat_ref[h, rows, KV_LORA:] = q_pe[:, h * QK_ROPE:(h + 1) * QK_ROPE]


def _project(x, cos, sin, tab_blocks, w, tm):
    n = x.shape[0]
    w1, w1r, qn, wuq, wuqr, kvn, wukt = w
    row = lambda width: pl.BlockSpec((tm, width), lambda i: (i, 0))
    heads = lambda width: pl.BlockSpec((8, tm, width), lambda i: (0, i, 0))
    tab = pl.BlockSpec((tm, N_TAB), lambda i: (i % tab_blocks, 0))
    return pl.pallas_call(
        _proj_kernel,
        grid=(n // tm,),
        in_specs=[row(D_MODEL), tab, tab, _resident(w1.shape), _resident(w1r.shape), _resident(qn.shape),
                  _resident(wuq.shape), _resident(wuqr.shape), _resident(kvn.shape), _resident(wukt.shape)],
        out_specs=[heads(A_HEAD_DIM), heads(IDX_DIM), row(IDX_HEADS), heads(ROW), row(ROW), row(ROW)],
        out_shape=[jax.ShapeDtypeStruct((A_HEADS, n, A_HEAD_DIM), F32),
                   jax.ShapeDtypeStruct((IDX_HEADS, n, IDX_DIM), F32),
                   jax.ShapeDtypeStruct((n, IDX_HEADS), F32),
                   jax.ShapeDtypeStruct((B_HEADS, n, ROW), F32),
                   jax.ShapeDtypeStruct((n, ROW), F32),
                   jax.ShapeDtypeStruct((n, ROW), F32)],
        compiler_params=_cparams("parallel"),
        name="project",
    )(x, cos, sin, w1, w1r, qn, wuq, wuqr, kvn, wukt)


def _merge_kernel(x_ref, oa_ref, olat_ref, wuv_ref, woa_ref, wob_ref, g_ref, b_ref, o_ref):
    ob = [_dot(olat_ref[:, h * KV_LORA:(h + 1) * KV_LORA], wuv_ref[h]) for h in range(B_HEADS)]
    ob = jnp.concatenate(ob, axis=-1).astype(BF16)
    mix = _dot(oa_ref[...], woa_ref[...]) + _dot(ob, wob_ref[...])
    o_ref[...] = _layernorm(DEEPNORM_ALPHA * x_ref[...] + mix, g_ref[...], b_ref[...])


def _merge(x, oa, olat, wuv, woa, wob, g, b, tm):
    n = x.shape[0]
    row = lambda width: pl.BlockSpec((tm, width), lambda i: (i, 0))
    return pl.pallas_call(
        _merge_kernel,
        grid=(n // tm,),
        in_specs=[row(D_MODEL), row(A_HEADS * A_HEAD_DIM), row(B_HEADS * KV_LORA), _resident(wuv.shape),
                  _resident(woa.shape), _resident(wob.shape), _resident(g.shape), _resident(b.shape)],
        out_specs=row(D_MODEL),
        out_shape=jax.ShapeDtypeStruct((n, D_MODEL), F32),
        compiler_params=_cparams("parallel"),
        name="merge",
    )(x, oa, olat, wuv, woa, wob, g, b)


LANE = 128
SELECT_BITS = 32
SELECT_UNROLL = 4


def _order_key(score):
    bits = lax.bitcast_convert_type(score + 0.0, jnp.int32)
    return bits ^ ((bits >> 31) & jnp.int32(0x7FFFFFFF))


def _count_ge(keys, cand):
    return jnp.sum(jnp.where(keys >= cand, 1.0, 0.0), axis=1, keepdims=True)


def _kth_largest_key(key_ref, top_k, unroll):
    rows = key_ref.shape[0]
    if unroll:
        t = jnp.full((rows, 1), INT_MIN, jnp.int32)
        for hi in range(SELECT_BITS - 1, 0, -2):
            keys = key_ref[...]
            b_hi = jnp.int32(INT_MIN if hi == 31 else 1 << hi)
            b_lo = jnp.int32(1 << (hi - 1))
            c1, c2 = t ^ b_lo, t ^ b_hi
            c3 = c2 ^ b_lo
            n1, n2, n3 = _count_ge(keys, c1), _count_ge(keys, c2), _count_ge(keys, c3)
            t = jnp.where(n3 >= top_k, c3, jnp.where(n2 >= top_k, c2, jnp.where(n1 >= top_k, c1, t)))
        return t

    def body(it, t):
        bit = lax.shift_left(jnp.int32(1), jnp.int32(SELECT_BITS - 1) - it)
        cand = t ^ bit
        return jnp.where(_count_ge(key_ref[...], cand) >= top_k, cand, t)

    return lax.fori_loop(0, SELECT_BITS, body, jnp.full((rows, 1), INT_MIN, jnp.int32), unroll=SELECT_UNROLL)


def _select_bias(key_ref, bias_ref, top_k, lane_order, unroll=False):
    kth = jnp.maximum(_kth_largest_key(key_ref, top_k, unroll), jnp.int32(INT_MIN + 1))
    keys = key_ref[...]
    bias_ref[...] = jnp.where(keys >= kth, 0.0, NEG_INF)
    n_ge = _count_ge(keys, kth)

    @pl.when(jnp.max(n_ge) > top_k)
    def _():
        ks = key_ref[...]
        need = top_k - jnp.sum(jnp.where(ks > kth, 1.0, 0.0), axis=1, keepdims=True)
        r_i = lax.broadcasted_iota(jnp.int32, (LANE, LANE), 0)
        c_i = lax.broadcasted_iota(jnp.int32, (LANE, LANE), 1)
        tri = jnp.where(r_i <= c_i, 1.0, 0.0).astype(BF16)
        seen = jnp.zeros_like(need)
        for blk in lane_order:
            sl = slice(blk * LANE, (blk + 1) * LANE)
            kb = key_ref[:, sl]
            tie = kb == kth
            tie_f = jnp.where(tie, 1.0, 0.0)
            rank = _dot(tie_f.astype(BF16), tri) + seen
            keep = (kb > kth) | (tie & (rank <= need))
            bias_ref[:, sl] = jnp.where(keep, 0.0, NEG_INF)
            seen = seen + jnp.sum(tie_f, axis=1, keepdims=True)


LOG2_E = 1.4426950408889634
MLA_SCALE_LOG2 = (QK_NOPE + QK_ROPE) ** -0.5 * LOG2_E


def _softmax_rows(s, base2=False):
    m = jnp.max(s, axis=-1, keepdims=True)
    p = jnp.exp2(s - m) if base2 else jnp.exp(s - m)
    return p, 1.0 / jnp.sum(p, axis=-1, keepdims=True)


def _prompt_positions(tq, n_real, q_pos0):
    lk = n_real + LANE
    lane = lax.broadcasted_iota(jnp.int32, (1, lk), 1)
    big = jnp.int32(2 ** 30)
    key_pos = jnp.where(lane < n_real, lane + N_META, jnp.where(lane < n_real + N_META, lane - n_real, big))
    q_pos = lax.broadcasted_iota(jnp.int32, (tq, 1), 0) + q_pos0
    return key_pos <= q_pos


def _gather_keys(real_ref, meta_ref, n_real):
    if n_real == 0:
        return meta_ref[...].astype(BF16)
    return jnp.concatenate([real_ref[...], meta_ref[...]], axis=0).astype(BF16)


def _dsa_prompt_kernel(qa_ref, qi_ref, wi_ref, real_ref, meta_ref, o_ref, key_ref, bias_ref, *,
                       tq, n_real, q_pos0, top_k):
    kb = _gather_keys(real_ref, meta_ref, n_real)
    k_idx = kb[:, A_KV:]
    wi = wi_ref[...]
    score = None
    for h in range(IDX_HEADS):
        d = jnp.maximum(_dot_nt(qi_ref[h].astype(BF16), k_idx), 0.0) * wi[:, h:h + 1]
        score = d if score is None else score + d
    adm = _prompt_positions(tq, n_real, q_pos0)
    key_ref[...] = jnp.where(adm, _order_key(score), jnp.int32(INT_MIN))
    n_blk = n_real // LANE
    _select_bias(key_ref, bias_ref, top_k, [n_blk] + list(range(n_blk)))
    k_att = kb[:, :A_HEAD_DIM]
    v_att = kb[:, A_HEAD_DIM:A_KV]
    for h in range(A_HEADS):
        s = _dot_nt((qa_ref[h] * (A_HEAD_DIM ** -0.5)).astype(BF16), k_att) + bias_ref[...]
        p, inv = _softmax_rows(s)
        o_ref[:, h * A_HEAD_DIM:(h + 1) * A_HEAD_DIM] = (_dot(p.astype(BF16), v_att) * inv).astype(o_ref.dtype)


def _mla_prompt_kernel(q_ref, real_ref, meta_ref, o_ref, *, tq, n_real, q_pos0):
    kb = _gather_keys(real_ref, meta_ref, n_real)
    c_kv = kb[:, :KV_LORA]
    bias = jnp.where(_prompt_positions(tq, n_real, q_pos0), 0.0, NEG_INF)
    for h in range(B_HEADS):
        s = _dot_nt(q_ref[h].astype(BF16), kb) * MLA_SCALE_LOG2 + bias
        p, inv = _softmax_rows(s, base2=True)
        o_ref[:, h * KV_LORA:(h + 1) * KV_LORA] = (_dot(p.astype(BF16), c_kv) * inv).astype(o_ref.dtype)


def _prompt_attention(qa, qi, wi, qcat, dsa_real, mla_real, dsa_meta, mla_meta, *, blk, tq, q_pos0, top_k):
    nb = qa.shape[1]
    has_real = dsa_real is not None
    n_real = (blk + 1) * tq if has_real else 0
    lk = n_real + LANE
    if not has_real:
        dsa_real = jnp.zeros((nb, 8, ROW), F32)
        mla_real = dsa_real
    nr_blk = max(n_real, 8)
    heads = lambda width: pl.BlockSpec((8, None, tq, width), lambda b: (0, b, blk, 0))
    rows = lambda width: pl.BlockSpec((None, tq, width), lambda b: (b, blk, 0))
    real = pl.BlockSpec((None, nr_blk, ROW), lambda b: (b, 0, 0))
    meta = _resident((LANE, ROW))
    pos0 = q_pos0 + blk * tq
    o_a = pl.pallas_call(
        functools.partial(_dsa_prompt_kernel, tq=tq, n_real=n_real, q_pos0=pos0, top_k=top_k),
        grid=(nb,),
        in_specs=[heads(A_HEAD_DIM), heads(IDX_DIM), rows(IDX_HEADS), real, meta],
        out_specs=pl.BlockSpec((None, tq, A_HEADS * A_HEAD_DIM), lambda b: (b, 0, 0)),
        out_shape=jax.ShapeDtypeStruct((nb, tq, A_HEADS * A_HEAD_DIM), BF16),
        scratch_shapes=[pltpu.VMEM((tq, lk), jnp.int32), pltpu.VMEM((tq, lk), F32)],
        compiler_params=_cparams("parallel"),
        name=f"dsa_prompt_{n_real}",
    )(qa, qi, wi, dsa_real, dsa_meta)
    o_lat = pl.pallas_call(
        functools.partial(_mla_prompt_kernel, tq=tq, n_real=n_real, q_pos0=pos0),
        grid=(nb,),
        in_specs=[heads(ROW), real, meta],
        out_specs=pl.BlockSpec((None, tq, B_HEADS * KV_LORA), lambda b: (b, 0, 0)),
        out_shape=jax.ShapeDtypeStruct((nb, tq, B_HEADS * KV_LORA), BF16),
        compiler_params=_cparams("parallel"),
        name=f"mla_prompt_{n_real}",
    )(qcat, mla_real, mla_meta)
    return o_a, o_lat


def _sample_kernel(pt_ref, qa_ref, qi_ref, wi_ref, qcat_ref, dsa_new_ref, mla_new_ref, dsa_hbm, mla_hbm,
                   oa_ref, olat_ref, dsa_buf, mla_buf, sems, key_ref, bias_ref, *, layer, n_pages, t_new, top_k):
    b = pl.program_id(0)
    nb = pl.num_programs(0)
    slot = b % 2
    past = n_pages * PAGE_SIZE
    lk = past + PAGE_SIZE

    def page_copies(seq, slt):
        out = []
        for p in range(n_pages):
            page = pt_ref[seq * n_pages + p]
            window = pl.ds(p * PAGE_SIZE, PAGE_SIZE)
            out.append(pltpu.make_async_copy(dsa_hbm.at[layer, page], dsa_buf.at[slt, :, window], sems.at[0, slt]))
            out.append(pltpu.make_async_copy(mla_hbm.at[layer, page], mla_buf.at[slt, :, window], sems.at[1, slt]))
        return out

    @pl.when(b == 0)
    def _():
        for c in page_copies(b, slot):
            c.start()

    @pl.when(b + 1 < nb)
    def _():
        for c in page_copies(b + 1, 1 - slot):
            c.start()

    dsa_buf[slot, :, past:] = dsa_new_ref[...]
    mla_buf[slot, :, past:] = mla_new_ref[...]

    for c in page_copies(b, slot):
        c.wait()

    n_row = A_HEADS * t_new
    lane = lax.broadcasted_iota(jnp.int32, (1, lk), 1)
    q_id = lax.broadcasted_iota(jnp.int32, (t_new, 1), 0)
    adm = lane <= q_id + past

    def per_head(bias):
        return jnp.concatenate([bias] * A_HEADS, axis=0)

    idx_t = dsa_buf[slot, A_KV:, :].astype(BF16)
    d = jnp.maximum(_dot(qi_ref[...].reshape(n_row, IDX_DIM).astype(BF16), idx_t), 0.0) * wi_ref[...]
    score = d[0:t_new]
    for h in range(1, IDX_HEADS):
        score = score + d[h * t_new:(h + 1) * t_new]
    key_ref[...] = jnp.where(adm, _order_key(score), jnp.int32(INT_MIN))

    km = mla_buf[slot].astype(BF16)
    s = _dot(qcat_ref[...].reshape(n_row, ROW).astype(BF16), km) * MLA_SCALE_LOG2
    p, inv = _softmax_rows(s + per_head(jnp.where(adm, 0.0, NEG_INF)), base2=True)
    o = _dot_nt(p.astype(BF16), km[:KV_LORA]) * inv
    for h in range(B_HEADS):
        olat_ref[:, h * KV_LORA:(h + 1) * KV_LORA] = o[h * t_new:(h + 1) * t_new]

    _select_bias(key_ref, bias_ref, top_k, list(range(lk // LANE)), unroll=True)

    k_t = dsa_buf[slot, :A_HEAD_DIM, :].astype(BF16)
    v_t = dsa_buf[slot, A_HEAD_DIM:A_KV, :].astype(BF16)
    qa = (qa_ref[...].reshape(n_row, A_HEAD_DIM) * (A_HEAD_DIM ** -0.5)).astype(BF16)
    p, inv = _softmax_rows(_dot(qa, k_t) + per_head(bias_ref[...]))
    o = _dot_nt(p.astype(BF16), v_t) * inv
    for h in range(A_HEADS):
        oa_ref[:, h * A_HEAD_DIM:(h + 1) * A_HEAD_DIM] = o[h * t_new:(h + 1) * t_new]


def _sample_attention(page_table, qa, qi, wi_col, qcat, dsa_new_t, mla_new_t, cache_dsa_t, cache_mla_t, *, layer,
                      top_k):
    nb, n_pages = page_table.shape
    t_new = qa.shape[2]
    lk = (n_pages + 1) * PAGE_SIZE
    heads = lambda width: pl.BlockSpec((8, None, t_new, width), lambda b, pt: (0, b, 0, 0))
    rows = lambda n, width: pl.BlockSpec((None, n, width), lambda b, pt: (b, 0, 0))
    grid_spec = pltpu.PrefetchScalarGridSpec(
        num_scalar_prefetch=1,
        grid=(nb,),
        in_specs=[heads(A_HEAD_DIM), heads(IDX_DIM), rows(8 * t_new, 1), heads(ROW), rows(ROW, PAGE_SIZE),
                  rows(ROW, PAGE_SIZE), pl.BlockSpec(memory_space=pl.ANY), pl.BlockSpec(memory_space=pl.ANY)],
        out_specs=[rows(t_new, A_HEADS * A_HEAD_DIM), rows(t_new, B_HEADS * KV_LORA)],
        scratch_shapes=[pltpu.VMEM((2, ROW, lk), F32),
                        pltpu.VMEM((2, ROW, lk), F32),
                        pltpu.SemaphoreType.DMA((2, 2)),
                        pltpu.VMEM((t_new, lk), jnp.int32),
                        pltpu.VMEM((t_new, lk), F32)],
    )
    return pl.pallas_call(
        functools.partial(_sample_kernel, layer=layer, n_pages=n_pages, t_new=t_new, top_k=top_k),
        grid_spec=grid_spec,
        out_shape=[jax.ShapeDtypeStruct((nb, t_new, A_HEADS * A_HEAD_DIM), F32),
                   jax.ShapeDtypeStruct((nb, t_new, B_HEADS * KV_LORA), F32)],
        compiler_params=_cparams("arbitrary"),
        name="sample_attention",
    )(page_table.reshape(-1), qa, qi, wi_col, qcat, dsa_new_t, mla_new_t, cache_dsa_t, cache_mla_t)


def _swap_signed(w, heads, head_dim, rot):
    k = w.shape[0]
    w3 = w.reshape(k, heads, head_dim)
    half = rot // 2
    rest = jnp.zeros((k, heads, head_dim - rot), w.dtype)
    return jnp.concatenate([-w3[:, :, half:rot], w3[:, :, :half], rest], axis=2).reshape(k, heads * head_dim)


def _prep_mixer_weights(w_in_l, q_norm_l, w_uq_l, kv_norm_l, w_uk_l, w_uv_l, w_o_l):
    col = lambda start, n: w_in_l[:, start:start + n]
    zero = lambda n: jnp.zeros((D_MODEL, n), w_in_l.dtype)
    aq, akv, iq, ik = col(O_AQ, 512), col(O_AKV, A_KV), col(O_IQ, 256), col(O_IK, IDX_DIM)
    iw, cq, ckv, kpe = col(O_IW, IDX_HEADS), col(O_CQ, Q_LORA), col(O_CKV, KV_LORA), col(O_KPE, QK_ROPE)
    gap, tail = zero(C_CKV - C_IK - IDX_DIM), zero(N1P - C_IW - IDX_HEADS)
    w1 = jnp.concatenate([aq, iq, cq, akv, ik, gap, ckv, kpe, iw, tail], axis=1)
    w1r = jnp.concatenate([_swap_signed(aq, A_HEADS, A_HEAD_DIM, A_ROT), _swap_signed(iq, IDX_HEADS, IDX_DIM, IDX_ROT),
                           zero(Q_LORA), _swap_signed(akv[:, :A_HEAD_DIM], 1, A_HEAD_DIM, A_ROT), zero(A_HEAD_DIM),
                           _swap_signed(ik, 1, IDX_DIM, IDX_ROT), gap, zero(KV_LORA),
                           _swap_signed(kpe, 1, QK_ROPE, QK_ROPE), zero(IDX_HEADS), tail], axis=1)
    uq = w_uq_l.reshape(Q_LORA, B_HEADS, QK_NOPE + QK_ROPE)
    uq_nope = uq[:, :, :QK_NOPE].reshape(Q_LORA, B_HEADS * QK_NOPE)
    uq_pe = uq[:, :, QK_NOPE:].reshape(Q_LORA, B_HEADS * QK_ROPE)
    uq_pe_r = _swap_signed(uq_pe, B_HEADS, QK_ROPE, QK_ROPE)
    wuq = jnp.concatenate([uq_nope, uq_pe], axis=1)
    wukt = jnp.transpose(w_uk_l, (1, 2, 0))
    wuv = jnp.transpose(w_uv_l, (1, 0, 2))
    n_a = A_HEADS * A_HEAD_DIM
    proj = (w1.astype(BF16), w1r.astype(BF16), q_norm_l.reshape(1, -1), wuq.astype(BF16), uq_pe_r.astype(BF16),
            kv_norm_l.reshape(1, -1), wukt.astype(BF16))
    merge = (wuv.astype(BF16), w_o_l[:n_a].astype(BF16), w_o_l[n_a:].astype(BF16))
    return proj, merge


def _prep_ffn_weights(w_in_l, w_out_l):
    return w_in_l[:, :D_FF].astype(BF16), w_in_l[:, D_FF:].astype(BF16), w_out_l.astype(BF16)


def _token_tile(n, cap):
    best = 8
    for t in range(8, cap + 1, 8):
        if n % t == 0:
            best = t
    return best


PROMPT_TQ = 256
PROMPT_TM = 512


def kernel(x_prompt, x_sample, cache_dsa, cache_mla, page_table, meta_tokens, ln1_g, ln1_b, ffn1_w_in, ffn1_w_out,
           w_in, mla_q_norm, mla_w_uq, mla_kv_norm, mla_w_uk, mla_w_uv, w_o, ln2_g, ln2_b, ffn2_w_in, ffn2_w_out,
           ln3_g, ln3_b):
    nb, seq, _ = x_prompt.shape
    db, t_new, _ = x_sample.shape
    n_pages = page_table.shape[1]
    past = n_pages * PAGE_SIZE
    n_p = nb * seq
    n_s = db * t_new
    n_g = n_s + N_META
    tq = min(PROMPT_TQ, seq)
    tm_p = _token_tile(n_p, PROMPT_TM)
    tm_p = tm_p if seq % tm_p == 0 else _token_tile(seq, PROMPT_TM)
    tm_g = _token_tile(n_g, 1024)
    topk_p = min(TOPK_MAX, (seq + N_META) // 4)
    topk_s = min(TOPK_MAX, (past + t_new) // 4)

    freq = _rope_freqs()
    pos_p = (jnp.arange(seq, dtype=jnp.int32) + N_META).astype(F32).reshape(seq, 1)
    pos_g = jnp.concatenate([jnp.tile(past + jnp.arange(t_new, dtype=jnp.int32), db),
                             jnp.arange(N_META, dtype=jnp.int32)]).astype(F32).reshape(n_g, 1)
    cos_p, sin_p = _rope_table(pos_p, freq, _token_tile(seq, 256))
    cos_g, sin_g = _rope_table(pos_g, freq, _token_tile(n_g, 256))

    cache_dsa_t = jnp.swapaxes(cache_dsa, 2, 3)
    cache_mla_t = jnp.swapaxes(cache_mla, 2, 3)

    xp = x_prompt.reshape(n_p, D_MODEL)
    xg = jnp.concatenate([x_sample.reshape(n_s, D_MODEL), meta_tokens.astype(x_prompt.dtype)], axis=0)
    vec = lambda v: v.reshape(1, -1)
    rows_out = [[], [], [], []]
    for l in range(DEPTH):
        f1 = _prep_ffn_weights(ffn1_w_in[l], ffn1_w_out[l])
        f2 = _prep_ffn_weights(ffn2_w_in[l], ffn2_w_out[l])
        pw, mw = _prep_mixer_weights(w_in[l], mla_q_norm[l], mla_w_uq[l], mla_kv_norm[l], mla_w_uk[l],
                                     mla_w_uv[l], w_o[l])
        xp = _ffn(xp, *f1, vec(ln1_g[l]), vec(ln1_b[l]), tm_p)
        xg = _ffn(xg, *f1, vec(ln1_g[l]), vec(ln1_b[l]), tm_g)

        qa_p, qi_p, wi_p, qcat_p, dsa_p, mla_p = _project(xp, cos_p, sin_p, seq // tm_p, pw, tm_p)
        qa_g, qi_g, wi_g, qcat_g, dsa_g, mla_g = _project(xg, cos_g, sin_g, n_g // tm_g, pw, tm_g)

        zpad = jnp.zeros((LANE - N_META, ROW), F32)
        dsa_meta = jnp.concatenate([dsa_g[n_s:], zpad], axis=0)
        mla_meta = jnp.concatenate([mla_g[n_s:], zpad], axis=0)
        hm = lambda a: a[:, n_s:].reshape(8, 1, N_META, a.shape[-1])
        oa_m, olat_m = _prompt_attention(hm(qa_g), hm(qi_g), wi_g[n_s:].reshape(1, N_META, IDX_HEADS), hm(qcat_g),
                                         None, None, dsa_meta, mla_meta, blk=0, tq=N_META, q_pos0=0, top_k=topk_p)

        hp = lambda a: a.reshape(8, nb, seq, a.shape[-1])
        dsa_p3 = dsa_p.reshape(nb, seq, ROW)
        mla_p3 = mla_p.reshape(nb, seq, ROW)
        oa_blocks, olat_blocks = [], []
        for blk in range(seq // tq):
            oa_b, olat_b = _prompt_attention(hp(qa_p), hp(qi_p), wi_p.reshape(nb, seq, IDX_HEADS), hp(qcat_p),
                                             dsa_p3, mla_p3, dsa_meta, mla_meta, blk=blk, tq=tq, q_pos0=N_META,
                                             top_k=topk_p)
            oa_blocks.append(oa_b)
            olat_blocks.append(olat_b)
        oa_p = jnp.concatenate(oa_blocks, axis=1).reshape(n_p, -1)
        olat_p = jnp.concatenate(olat_blocks, axis=1).reshape(n_p, -1)

        hs = lambda a: a[:, :n_s].reshape(8, db, t_new, a.shape[-1])
        wi_col = jnp.transpose(wi_g[:n_s].reshape(db, t_new, IDX_HEADS), (0, 2, 1)).reshape(db, IDX_HEADS * t_new, 1)
        dsa_s = dsa_g[:n_s].reshape(db, t_new, ROW)
        mla_s = mla_g[:n_s].reshape(db, t_new, ROW)
        new_t = lambda r: jnp.pad(jnp.swapaxes(r, 1, 2), ((0, 0), (0, 0), (0, PAGE_SIZE - t_new)))
        oa_s, olat_s = _sample_attention(page_table, hs(qa_g), hs(qi_g), wi_col, hs(qcat_g), new_t(dsa_s),
                                         new_t(mla_s), cache_dsa_t, cache_mla_t, layer=l, top_k=topk_s)
        oa_g = jnp.concatenate([oa_s.reshape(n_s, -1).astype(BF16), oa_m.reshape(N_META, -1)], axis=0)
        olat_g = jnp.concatenate([olat_s.reshape(n_s, -1).astype(BF16), olat_m.reshape(N_META, -1)], axis=0)

        xp = _merge(xp, oa_p, olat_p, *mw, vec(ln2_g[l]), vec(ln2_b[l]), tm_p)
        xg = _merge(xg, oa_g, olat_g, *mw, vec(ln2_g[l]), vec(ln2_b[l]), tm_g)
        xp = _ffn(xp, *f2, vec(ln3_g[l]), vec(ln3_b[l]), tm_p)
        xg = _ffn(xg, *f2, vec(ln3_g[l]), vec(ln3_b[l]), tm_g)

        bc = lambda m: jnp.broadcast_to(m[None, :N_META], (nb, N_META, ROW))
        rows_out[0].append(jnp.concatenate([bc(dsa_meta), dsa_p3], axis=1))
        rows_out[1].append(jnp.concatenate([bc(mla_meta), mla_p3], axis=1))
        rows_out[2].append(dsa_s)
        rows_out[3].append(mla_s)

    y_prompt = xp.reshape(nb, seq, D_MODEL)
    y_sample = xg[:n_s].reshape(db, t_new, D_MODEL)
    return (y_prompt, y_sample, jnp.stack(rows_out[0]), jnp.stack(rows_out[1]), jnp.stack(rows_out[2]),
            jnp.stack(rows_out[3]))
```
